```python
import math
import jax, jax.numpy as jnp
from jax import lax
import numpy as np

D_MODEL = 4096
BATCH = 2
SEQ = 4096
DEPTH = 2

HEAD_DIM = 128
N_SLOTS = D_MODEL // HEAD_DIM
A_HEADS = N_SLOTS // 8
B_HEADS = (N_SLOTS - 2 * A_HEADS) // 2
C_HEADS = N_SLOTS - 2 * A_HEADS - B_HEADS
A_VDIM = 2 * HEAD_DIM
A_WIDTH = A_HEADS * A_VDIM
B_WIDTH = B_HEADS * HEAD_DIM
C_WIDTH = C_HEADS * HEAD_DIM
MIX_WIDTH = A_WIDTH + B_WIDTH + C_WIDTH
IN_SPLITS = [A_WIDTH] * 3 + [B_WIDTH] * 3 + [C_WIDTH] * 3
IN_WIDTH = sum(IN_SPLITS)
D_FF = -(-8 * D_MODEL // (3 * 256)) * 256
GRID_W = 64
NA_ROWS_MAX = 8
NA_COLS = 16
Q_BLOCK = 128
DILATED_BRANCHES = ((128, 1), (512, 4), (2048, 16))
C_QBLOCK = 64
RMS_EPS = 1e-6

kernel_name = "hybrid_diff_natten_dilated_encoder"


def rmsnorm(x, g):
    xf = x.astype(jnp.float32)
    y = xf * lax.rsqrt(jnp.mean(xf * xf, axis=-1, keepdims=True) + RMS_EPS)
    return (y * g.astype(jnp.float32)).astype(x.dtype)


def alibi_slopes(n):
    return jnp.exp2(-8.0 * jnp.arange(1, n + 1, dtype=jnp.float32) / n)


def diff_attention(q, k, v, lam, slopes):
    b_, s_len, h_, _, d = q.shape
    scale = d ** -0.5
    qh = jnp.transpose(q, (0, 2, 3, 1, 4))
    kh = jnp.transpose(k, (0, 2, 3, 1, 4))
    vh = jnp.transpose(v, (0, 2, 1, 3))
    pos = jnp.arange(s_len)

    def block(i):
        start = i * Q_BLOCK
        qb = lax.dynamic_slice_in_dim(qh, start, Q_BLOCK, axis=3)
        s = jnp.einsum('bhmqd,bhmkd->bhmqk', qb, kh, preferred_element_type=jnp.float32) * scale
        qpos = start + jnp.arange(Q_BLOCK)
        dist = jnp.abs(qpos[:, None] - pos[None, :]).astype(jnp.float32)
        s = s - slopes[None, :, None, None, None] * dist
        p = jax.nn.softmax(s, axis=-1)
        pdiff = p[:, :, 0] - lam * p[:, :, 1]
        return jnp.einsum('bhqk,bhkv->bhqv', pdiff.astype(v.dtype), vh)

    out = lax.map(block, jnp.arange(s_len // Q_BLOCK))
    return jnp.transpose(out, (1, 0, 3, 2, 4)).reshape(b_, s_len, h_, v.shape[-1])


def neighborhood_attention(q, k, v, rpb):
    b_, s_len, h_, d = q.shape
    rows = s_len // GRID_W
    kh_ = min(NA_ROWS_MAX, rows)
    kw_ = NA_COLS
    scale = d ** -0.5

    def grid(t):
        return jnp.transpose(t.reshape(b_, rows, GRID_W, h_, d), (0, 3, 1, 2, 4))

    qg, kg, vg = grid(q), grid(k), grid(v)
    cols = jnp.arange(GRID_W)
    c_start = jnp.clip(cols - kw_ // 2, 0, GRID_W - kw_)
    col_mask = (cols[None, :] >= c_start[:, None]) & (cols[None, :] < c_start[:, None] + kw_)
    dc_idx = jnp.clip(cols[None, :] - cols[:, None] + NA_COLS - 1, 0, 2 * NA_COLS - 2)
    rpb_c = rpb[:, :, dc_idx]

    def row(r):
        r_start = jnp.clip(r - kh_ // 2, 0, rows - kh_)
        qr = lax.dynamic_index_in_dim(qg, r, axis=2, keepdims=False)
        kb = lax.dynamic_slice_in_dim(kg, r_start, kh_, axis=2)
        vb = lax.dynamic_slice_in_dim(vg, r_start, kh_, axis=2)
        s = jnp.einsum('bhcd,bhijd->bhcij', qr, kb, preferred_element_type=jnp.float32) * scale
        dr_idx = r_start + jnp.arange(kh_) - r + NA_ROWS_MAX - 1
        bias = jnp.transpose(rpb_c[:, dr_idx], (0, 2, 1, 3)).astype(jnp.float32)
        s = jnp.where(col_mask[:, None, :], s + bias[None], -jnp.inf)
        p = jax.nn.softmax(s.reshape(b_, h_, GRID_W, kh_ * GRID_W), axis=-1)
        return jnp.einsum('bhcn,bhnd->bhcd', p.astype(v.dtype), vb.reshape(b_, h_, kh_ * GRID_W, d))

    out = lax.map(row, jnp.arange(rows))
    return jnp.transpose(out, (1, 0, 3, 2, 4)).reshape(b_, s_len, h_, d)


def dilated_attention(q, k, v, slopes):
    b_, s_len, h_, d = q.shape
    scale = d ** -0.5
    qh, kh, vh = [jnp.transpose(t, (0, 2, 1, 3)) for t in (q, k, v)]
    outs, lses = [], []
    for window, dil in DILATED_BRANCHES:
        radius = window // (2 * dil)
        L = s_len // dil
        qb_size = math.gcd(L, C_QBLOCK)
        n_blk = L // qb_size
        qs = jnp.transpose(qh.reshape(b_, h_, L, dil, d), (0, 1, 3, 2, 4)).reshape(b_, h_, dil, n_blk, qb_size, d)
        ks = jnp.transpose(kh.reshape(b_, h_, L, dil, d), (0, 1, 3, 2, 4))
        vs = jnp.transpose(vh.reshape(b_, h_, L, dil, d), (0, 1, 3, 2, 4))
        u_q = jnp.arange(n_blk)[:, None] * qb_size + jnp.arange(qb_size)[None, :]
        u_k = jnp.arange(n_blk)[:, None] * qb_size - radius + jnp.arange(qb_size + 2 * radius)[None, :]
        valid = (u_k >= 0) & (u_k < L)
        u_kc = jnp.clip(u_k, 0, L - 1)
        kb = jnp.take(ks, u_kc, axis=3)
        vb = jnp.take(vs, u_kc, axis=3)
        s = jnp.einsum('bhrnqd,bhrnkd->bhrnqk', qs, kb, preferred_element_type=jnp.float32) * scale
        du = jnp.abs(u_k[:, None, :] - u_q[:, :, None])
        mask = valid[:, None, :] & (du <= radius)
        s = s - slopes[None, :, None, None, None, None] * (du * dil).astype(jnp.float32)
        s = jnp.where(mask, s, -jnp.inf)
        lse = jax.nn.logsumexp(s, axis=-1)
        p = jnp.exp(s - lse[..., None])
        o = jnp.einsum('bhrnqk,bhrnkd->bhrnqd', p.astype(v.dtype), vb)
        o = jnp.transpose(o.reshape(b_, h_, dil, L, d), (0, 1, 3, 2, 4)).reshape(b_, h_, s_len, d)
        lse = jnp.transpose(lse.reshape(b_, h_, dil, L), (0, 1, 3, 2)).reshape(b_, h_, s_len)
        outs.append(o)
        lses.append(lse)
    w = jax.nn.softmax(jnp.stack(lses, axis=0), axis=0)
    out = jnp.einsum('gbhs,gbhsd->bhsd', w.astype(v.dtype), jnp.stack(outs, axis=0))
    return jnp.transpose(out, (0, 2, 1, 3))


def setup_inputs(seed: int = 0) -> dict:
    key = jax.random.key(seed)
    ks = jax.random.split(key, 24)
    f32 = jnp.float32

    def normal(k, shape, scale):
        return jax.random.normal(k, shape, dtype=f32) * scale

    def gain(k, shape):
        return 1.0 + 0.02 * jax.random.normal(k, shape, dtype=f32)

    n_rel_r = 2 * NA_ROWS_MAX - 1
    n_rel_c = 2 * NA_COLS - 1
    return {
        "x": normal(ks[0], (BATCH, SEQ, D_MODEL), 1.0),
        "norm1_g": gain(ks[1], (DEPTH, D_MODEL)),
        "w_in": normal(ks[2], (DEPTH, D_MODEL, IN_WIDTH), D_MODEL ** -0.5),
        "a_q_g": gain(ks[3], (DEPTH, HEAD_DIM)),
        "a_k_g": gain(ks[4], (DEPTH, HEAD_DIM)),
        "lambda_q1": normal(ks[5], (DEPTH, HEAD_DIM), 0.1),
        "lambda_k1": normal(ks[6], (DEPTH, HEAD_DIM), 0.1),
        "lambda_q2": normal(ks[7], (DEPTH, HEAD_DIM), 0.1),
        "lambda_k2": normal(ks[8], (DEPTH, HEAD_DIM), 0.1),
        "a_out_g": gain(ks[9], (DEPTH, A_VDIM)),
        "b_q_g": gain(ks[10], (DEPTH, HEAD_DIM)),
        "b_k_g": gain(ks[11], (DEPTH, HEAD_DIM)),
        "b_rpb": normal(ks[12], (DEPTH, B_HEADS, n_rel_r, n_rel_c), 0.02),
        "b_out_g": gain(ks[13], (DEPTH, HEAD_DIM)),
        "c_q_g": gain(ks[14], (DEPTH, HEAD_DIM)),
        "c_k_g": gain(ks[15], (DEPTH, HEAD_DIM)),
        "c_out_g": gain(ks[16], (DEPTH, HEAD_DIM)),
        "w_out": normal(ks[17], (DEPTH, MIX_WIDTH, D_MODEL), MIX_WIDTH ** -0.5),
        "norm2_g": gain(ks[18], (DEPTH, D_MODEL)),
        "w_gate": normal(ks[19], (DEPTH, D_MODEL, D_FF), D_MODEL ** -0.5),
        "w_up": normal(ks[20], (DEPTH, D_MODEL, D_FF), D_MODEL ** -0.5),
        "w_down": normal(ks[21], (DEPTH, D_FF, D_MODEL), D_FF ** -0.5),
    }


def reference(x, norm1_g, w_in, a_q_g, a_k_g, lambda_q1, lambda_k1, lambda_q2, lambda_k2, a_out_g,
              b_q_g, b_k_g, b_rpb, b_out_g, c_q_g, c_k_g, c_out_g, w_out, norm2_g, w_gate, w_up, w_down):
    b_, s_len, _ = x.shape
    slopes_a = alibi_slopes(A_HEADS)
    slopes_c = alibi_slopes(C_HEADS)
    split_points = [int(p) for p in np.cumsum(IN_SPLITS)[:-1]]
    for l in range(DEPTH):
        h = rmsnorm(x, norm1_g[l])
        proj = h @ w_in[l]
        qa, ka, va, qb, kb, vb, qc, kc, vc = jnp.split(proj, split_points, axis=-1)

        qa = rmsnorm(qa.reshape(b_, s_len, A_HEADS, 2, HEAD_DIM), a_q_g[l])
        ka = rmsnorm(ka.reshape(b_, s_len, A_HEADS, 2, HEAD_DIM), a_k_g[l])
        va = va.reshape(b_, s_len, A_HEADS, A_VDIM)
        lam_init = 0.8 - 0.6 * math.exp(-0.3 * l)
        lam = (jnp.exp(jnp.sum(lambda_q1[l].astype(jnp.float32) * lambda_k1[l].astype(jnp.float32)))
               - jnp.exp(jnp.sum(lambda_q2[l].astype(jnp.float32) * lambda_k2[l].astype(jnp.float32)))
               + lam_init)
        oa = diff_attention(qa, ka, va, lam, slopes_a)
        oa = rmsnorm(oa, a_out_g[l]) * (1.0 - lam_init)

        qb = rmsnorm(qb.reshape(b_, s_len, B_HEADS, HEAD_DIM), b_q_g[l])
        kb = rmsnorm(kb.reshape(b_, s_len, B_HEADS, HEAD_DIM), b_k_g[l])
        vb = vb.reshape(b_, s_len, B_HEADS, HEAD_DIM)
        ob = rmsnorm(neighborhood_attention(qb, kb, vb, b_rpb[l]), b_out_g[l])

        qc = rmsnorm(qc.reshape(b_, s_len, C_HEADS, HEAD_DIM), c_q_g[l])
        kc = rmsnorm(kc.reshape(b_, s_len, C_HEADS, HEAD_DIM), c_k_g[l])
        vc = vc.reshape(b_, s_len, C_HEADS, HEAD_DIM)
        oc = rmsnorm(dilated_attention(qc, kc, vc, slopes_c), c_out_g[l])

        mix = jnp.concatenate([oa.reshape(b_, s_len, A_WIDTH),
                               ob.reshape(b_, s_len, B_WIDTH),
                               oc.reshape(b_, s_len, C_WIDTH)], axis=-1)
        x = x + mix @ w_out[l]

        h2 = rmsnorm(x, norm2_g[l])
        x = x + (jax.nn.silu(h2 @ w_gate[l]) * (h2 @ w_up[l])) @ w_down[l]
    return x
```

```python
import functools
import math

import jax
import jax.numpy as jnp
from jax import lax
from jax.experimental import pallas as pl
from jax.experimental.pallas import tpu as pltpu

F32 = jnp.float32
BF16 = jnp.bfloat16

HEAD_DIM = 128
GRID_W = 64
NA_ROWS = 8
NA_COLS = 16
DILATED_BRANCHES = ((128, 1), (512, 4), (2048, 16))
RMS_EPS = 1e-6
NEG = -1e30

VMEM_LIMIT = 56 * 1024 * 1024

_NT = (((1,), (1,)), ((), ()))


def _rms(y):
    return y * lax.rsqrt(jnp.mean(y * y, axis=-1, keepdims=True) + RMS_EPS)


def _params(sem):
    return pltpu.CompilerParams(dimension_semantics=sem, vmem_limit_bytes=VMEM_LIMIT)


def _rmsnorm_kernel(x_ref, g_ref, o_ref):
    o_ref[...] = (_rms(x_ref[...]) * g_ref[...]).astype(o_ref.dtype)


def _rmsnorm(x2, g, tm=256):
    m, d = x2.shape
    return pl.pallas_call(
        _rmsnorm_kernel,
        grid=(m // tm,),
        in_specs=[pl.BlockSpec((tm, d), lambda i: (i, 0)),
                  pl.BlockSpec((1, d), lambda i: (0, 0))],
        out_specs=pl.BlockSpec((tm, d), lambda i: (i, 0)),
        out_shape=jax.ShapeDtypeStruct((m, d), BF16),
        compiler_params=_params(("arbitrary",)),
        name="rmsnorm",
    )(x2, g.reshape(1, d))


def _inproj_kernel(h_ref, w_ref, g_ref, o_ref, *, norm_tiles, tn):
    j = pl.program_id(1)
    acc = jnp.dot(h_ref[...], w_ref[...], preferred_element_type=F32)
    is_norm = functools.reduce(jnp.logical_or, [(j >= lo) & (j < hi) for lo, hi in norm_tiles])

    @pl.when(is_norm)
    def _():
        for c in range(tn // HEAD_DIM):
            sl = slice(c * HEAD_DIM, (c + 1) * HEAD_DIM)
            o_ref[:, sl] = (_rms(acc[:, sl]) * g_ref[:, sl]).astype(o_ref.dtype)

    @pl.when(jnp.logical_not(is_norm))
    def _():
        o_ref[...] = acc.astype(o_ref.dtype)


def _inproj(h, w, gcols, norm_tiles, tm=1024, tn=512):
    m, k = h.shape
    n = w.shape[1]
    return pl.pallas_call(
        functools.partial(_inproj_kernel, norm_tiles=norm_tiles, tn=tn),
        grid=(m // tm, n // tn),
        in_specs=[pl.BlockSpec((tm, k), lambda i, j: (i, 0)),
                  pl.BlockSpec((k, tn), lambda i, j: (0, j)),
                  pl.BlockSpec((1, tn), lambda i, j: (0, j))],
        out_specs=pl.BlockSpec((tm, tn), lambda i, j: (i, j)),
        out_shape=jax.ShapeDtypeStruct((m, n), BF16),
        compiler_params=_params(("arbitrary", "arbitrary")),
        name="inproj",
    )(h, w, gcols)


def _attn_a_kernel(slopes_ref, lamv_ref, q_ref, k_ref, v_ref, g_ref, o_ref, tb_ref,
                   *, tq, tk, seq, lam_init):
    h = pl.program_id(1)
    qi = pl.program_id(2)
    nq = seq // tq

    @pl.when(qi == 0)
    def _():
        slope = slopes_ref[h]
        d0 = (lax.broadcasted_iota(jnp.int32, (tq, tq), 1)
              - lax.broadcasted_iota(jnp.int32, (tq, tq), 0))
        for cx in range(2 * nq - 1):
            delta = d0 + (cx - (nq - 1)) * tq
            tb_ref[cx] = jnp.abs(delta).astype(F32) * (-slope)

    lv = lamv_ref[...]
    lam = (jnp.exp(jnp.sum(lv[0:1] * lv[1:2], axis=-1, keepdims=True))
           - jnp.exp(jnp.sum(lv[2:3] * lv[3:4], axis=-1, keepdims=True)) + lam_init)

    q = q_ref[...]
    m_run = [None, None]
    l_run = [None, None]
    acc = [None, None]
    sub = tk // tq
    for c in range(seq // tk):
        bias = jnp.concatenate([tb_ref[(c * sub + t) - qi + (nq - 1)] for t in range(sub)], axis=1)
        v = v_ref[c * tk:(c + 1) * tk, :]
        for m in range(2):
            hs = slice(m * HEAD_DIM, (m + 1) * HEAD_DIM)
            s = lax.dot_general(q[:, hs], k_ref[c * tk:(c + 1) * tk, hs], _NT,
                                preferred_element_type=F32) + bias
            mx = jnp.max(s, axis=-1, keepdims=True)
            if c == 0:
                m_new = mx
                p = jnp.exp(s - m_new)
                l_run[m] = jnp.sum(p, axis=-1, keepdims=True)
                acc[m] = jnp.dot(p.astype(BF16), v, preferred_element_type=F32)
            else:
                m_new = jnp.maximum(m_run[m], mx)
                alpha = jnp.exp(m_run[m] - m_new)
                p = jnp.exp(s - m_new)
                l_run[m] = alpha * l_run[m] + jnp.sum(p, axis=-1, keepdims=True)
                acc[m] = alpha * acc[m] + jnp.dot(p.astype(BF16), v, preferred_element_type=F32)
            m_run[m] = m_new
    o = acc[0] * (1.0 / l_run[0]) - acc[1] * (lam / l_run[1])
    o_ref[...] = (_rms(o) * (g_ref[...] * (1.0 - lam_init))).astype(o_ref.dtype)


def _attn_a(proj, slopes, lamv, g_out, *, batch, seq, heads, col0, lam_init, tq=256, tk=512):
    vd = 2 * HEAD_DIM
    nq = seq // tq
    qb, kb, vb = (col0 // vd, col0 // vd + heads, col0 // vd + 2 * heads)
    smem = pl.BlockSpec(memory_space=pltpu.SMEM)
    return pl.pallas_call(
        functools.partial(_attn_a_kernel, tq=tq, tk=tk, seq=seq, lam_init=lam_init),
        grid=(batch, heads, nq),
        in_specs=[smem,
                  pl.BlockSpec((4, HEAD_DIM), lambda b, h, i: (0, 0)),
                  pl.BlockSpec((tq, vd), lambda b, h, i: (b * nq + i, qb + h)),
                  pl.BlockSpec((seq, vd), lambda b, h, i: (b, kb + h)),
                  pl.BlockSpec((seq, vd), lambda b, h, i: (b, vb + h)),
                  pl.BlockSpec((1, vd), lambda b, h, i: (0, 0))],
        out_specs=pl.BlockSpec((tq, vd), lambda b, h, i: (b * nq + i, h)),
        out_shape=jax.ShapeDtypeStruct((batch * seq, heads * vd), BF16),
        scratch_shapes=[pltpu.VMEM((2 * nq - 1, tq, tq), F32)],
        compiler_params=_params(("arbitrary", "arbitrary", "arbitrary")),
        name="attn_a",
    )(slopes, lamv, proj, proj, proj, g_out.reshape(1, vd))


def _attn_b_kernel(q_ref, k_ref, v_ref, tab_ref, g_ref, o_ref, *, rows, unroll):
    half = NA_ROWS // 2
    win = NA_ROWS * GRID_W

    def body(r, carry):
        rs = jnp.clip(r - half, 0, rows - NA_ROWS)
        q = q_ref[pl.ds(pl.multiple_of(r * GRID_W, GRID_W), GRID_W), :]
        k0 = pl.multiple_of(rs * GRID_W, GRID_W)
        k = k_ref[pl.ds(k0, win), :]
        v = v_ref[pl.ds(k0, win), :]
        s = lax.dot_general(q, k, _NT, preferred_element_type=F32) + tab_ref[0, r - rs]
        p = jnp.exp(s - jnp.max(s, axis=-1, keepdims=True))
        l = jnp.sum(p, axis=-1, keepdims=True)
        o = jnp.dot(p.astype(BF16), v, preferred_element_type=F32) * (1.0 / l)
        o_ref[pl.ds(pl.multiple_of(r * GRID_W, GRID_W), GRID_W), :] = (
            _rms(o) * g_ref[...]).astype(o_ref.dtype)
        return carry

    lax.fori_loop(0, rows, body, 0, unroll=unroll)


def _attn_b(proj, tab, g_out, *, batch, seq, heads, col0, unroll=4):
    rows = seq // GRID_W
    qb = col0 // HEAD_DIM
    kb, vb = qb + heads, qb + 2 * heads
    return pl.pallas_call(
        functools.partial(_attn_b_kernel, rows=rows, unroll=unroll),
        grid=(batch, heads),
        in_specs=[pl.BlockSpec((seq, HEAD_DIM), lambda b, h: (b, qb + h)),
                  pl.BlockSpec((seq, HEAD_DIM), lambda b, h: (b, kb + h)),
                  pl.BlockSpec((seq, HEAD_DIM), lambda b, h: (b, vb + h)),
                  pl.BlockSpec((1, NA_ROWS, GRID_W, NA_ROWS * GRID_W), lambda b, h: (h, 0, 0, 0)),
                  pl.BlockSpec((1, HEAD_DIM), lambda b, h: (0, 0))],
        out_specs=pl.BlockSpec((seq, HEAD_DIM), lambda b, h: (b, h)),
        out_shape=jax.ShapeDtypeStruct((batch * seq, heads * HEAD_DIM), BF16),
        compiler_params=_params(("arbitrary", "arbitrary")),
        name="attn_b",
    )(proj, proj, proj, tab, g_out.reshape(1, HEAD_DIM))


def _nbr_bias_table(rpb):
    cols = jnp.arange(GRID_W)
    c_start = jnp.clip(cols - NA_COLS // 2, 0, GRID_W - NA_COLS)
    col_mask = (cols[None, :] >= c_start[:, None]) & (cols[None, :] < c_start[:, None] + NA_COLS)
    dc = jnp.clip(cols[None, :] - cols[:, None] + NA_COLS - 1, 0, 2 * NA_COLS - 2)
    dr = jnp.arange(NA_ROWS)[None, :] - jnp.arange(NA_ROWS)[:, None] + NA_ROWS - 1
    t = rpb[:, dr][..., dc]
    t = jnp.where(col_mask[None, None, None], t.astype(F32), NEG)
    t = jnp.transpose(t, (0, 1, 3, 2, 4))
    return t.reshape(rpb.shape[0], NA_ROWS, GRID_W, NA_ROWS * GRID_W)


def _attn_c_kernel(slopes_ref, q_ref, k_ref, v_ref, g_ref, o_ref, tb_ref, *, tq, seq, nwin):
    h = pl.program_id(1)
    qi = pl.program_id(2)
    nq = seq // tq
    reach = (nwin - 1) // 2

    @pl.when(qi == 0)
    def _():
        slope = slopes_ref[h]
        d0 = (lax.broadcasted_iota(jnp.int32, (tq, tq), 1)
              - lax.broadcasted_iota(jnp.int32, (tq, tq), 0))
        for cx in range(2 * (nwin - 1) + 1):
            delta = d0 + (cx - (nwin - 1)) * tq
            ad = jnp.abs(delta)
            mult = jnp.zeros((tq, tq), F32)
            for window, dil in DILATED_BRANCHES:
                radius = window // (2 * dil)
                hit = ((delta & (dil - 1)) == 0) & (ad <= radius * dil)
                mult = mult + hit.astype(F32)
            logm = jnp.where(mult > 0, jnp.log(jnp.maximum(mult, 1.0)), NEG)
            tb_ref[cx] = ad.astype(F32) * (-slope) + logm

    ks = jnp.clip(qi - reach, 0, nq - nwin)
    k0 = pl.multiple_of(ks * tq, tq)
    bias = jnp.concatenate([tb_ref[ks + t - qi + (nwin - 1)] for t in range(nwin)], axis=1)
    s = lax.dot_general(q_ref[...], k_ref[pl.ds(k0, nwin * tq), :], _NT,
                        preferred_element_type=F32) + bias
    p = jnp.exp(s - jnp.max(s, axis=-1, keepdims=True))
    l = jnp.sum(p, axis=-1, keepdims=True)
    o = jnp.dot(p.astype(BF16), v_ref[pl.ds(k0, nwin * tq), :],
                preferred_element_type=F32) * (1.0 / l)
    o_ref[...] = (_rms(o) * g_ref[...]).astype(o_ref.dtype)


def _attn_c(proj, slopes, g_out, *, batch, seq, heads, col0, tq=256):
    for _, dil in DILATED_BRANCHES:
        assert dil & (dil - 1) == 0
    span = max((w // (2 * d)) * d for w, d in DILATED_BRANCHES)
    nwin = 2 * (span // tq) + 1
    assert span % tq == 0 and seq // tq >= nwin
    nq = seq // tq
    qb = col0 // HEAD_DIM
    kb, vb = qb + heads, qb + 2 * heads
    smem = pl.BlockSpec(memory_space=pltpu.SMEM)
    return pl.pallas_call(
        functools.partial(_attn_c_kernel, tq=tq, seq=seq, nwin=nwin),
        grid=(batch, heads, nq),
        in_specs=[smem,
                  pl.BlockSpec((tq, HEAD_DIM), lambda b, h, i: (b * nq + i, qb + h)),
                  pl.BlockSpec((seq, HEAD_DIM), lambda b, h, i: (b, kb + h)),
                  pl.BlockSpec((seq, HEAD_DIM), lambda b, h, i: (b, vb + h)),
                  pl.BlockSpec((1, HEAD_DIM), lambda b, h, i: (0, 0))],
        out_specs=pl.BlockSpec((tq, HEAD_DIM), lambda b, h, i: (b * nq + i, h)),
        out_shape=jax.ShapeDtypeStruct((batch * seq, heads * HEAD_DIM), BF16),
        scratch_shapes=[pltpu.VMEM((2 * (nwin - 1) + 1, tq, tq), F32)],
        compiler_params=_params(("arbitrary", "arbitrary", "arbitrary")),
        name="attn_c",
    )(slopes, proj, proj, proj, g_out.reshape(1, HEAD_DIM))


def _outproj_kernel(a_ref, b_ref, c_ref, w_ref, x_ref, o_ref, *, splits):
    acc = x_ref[...]
    row = 0
    for part, width in zip((a_ref, b_ref, c_ref), splits):
        acc = acc + jnp.dot(part[...], w_ref[row:row + width, :], preferred_element_type=F32)
        row += width
    o_ref[...] = acc


def _outproj(ma, mb, mc, w, x2, tm=1024, tn=512):
    m, n = x2.shape
    splits = (ma.shape[1], mb.shape[1], mc.shape[1])
    k = sum(splits)
    return pl.pallas_call(
        functools.partial(_outproj_kernel, splits=splits),
        grid=(m // tm, n // tn),
        in_specs=[pl.BlockSpec((tm, splits[0]), lambda i, j: (i, 0)),
                  pl.BlockSpec((tm, splits[1]), lambda i, j: (i, 0)),
                  pl.BlockSpec((tm, splits[2]), lambda i, j: (i, 0)),
                  pl.BlockSpec((k, tn), lambda i, j: (0, j)),
                  pl.BlockSpec((tm, tn), lambda i, j: (i, j))],
        out_specs=pl.BlockSpec((tm, tn), lambda i, j: (i, j)),
        out_shape=jax.ShapeDtypeStruct((m, n), F32),
        compiler_params=_params(("arbitrary", "arbitrary")),
        name="outproj",
    )(ma, mb, mc, w, x2)


def _ffn_kernel(x_ref, g_ref, wg_ref, wu_ref, wd_ref, o_ref, h_ref):
    j = pl.program_id(1)

    @pl.when(j == 0)
    def _():
        x = x_ref[...]
        h_ref[...] = (_rms(x) * g_ref[...]).astype(h_ref.dtype)
        o_ref[...] = x

    h = h_ref[...]
    gate = jnp.dot(h, wg_ref[...], preferred_element_type=F32)
    up = jnp.dot(h, wu_ref[...], preferred_element_type=F32)
    act = (gate * jax.nn.sigmoid(gate) * up).astype(BF16)
    o_ref[...] += jnp.dot(act, wd_ref[...], preferred_element_type=F32)


def _ffn(x2, g, wg, wu, wd, tm=512, tf=256):
    m, d = x2.shape
    f = wg.shape[1]
    return pl.pallas_call(
        _ffn_kernel,
        grid=(m // tm, f // tf),
        in_specs=[pl.BlockSpec((tm, d), lambda i, j: (i, 0), pipeline_mode=pl.Buffered(1)),
                  pl.BlockSpec((1, d), lambda i, j: (0, 0)),
                  pl.BlockSpec((d, tf), lambda i, j: (0, j)),
                  pl.BlockSpec((d, tf), lambda i, j: (0, j)),
                  pl.BlockSpec((tf, d), lambda i, j: (j, 0))],
        out_specs=pl.BlockSpec((tm, d), lambda i, j: (i, 0)),
        out_shape=jax.ShapeDtypeStruct((m, d), F32),
        scratch_shapes=[pltpu.VMEM((tm, d), BF16)],
        compiler_params=_params(("arbitrary", "arbitrary")),
        name="ffn",
    )(x2, g.reshape(1, d), wg, wu, wd)


def _alibi_slopes(n):
    return jnp.exp2(-8.0 * jnp.arange(1, n + 1, dtype=F32) / n)


def kernel(x, norm1_g, w_in, a_q_g, a_k_g, lambda_q1, lambda_k1, lambda_q2, lambda_k2, a_out_g,
           b_q_g, b_k_g, b_rpb, b_out_g, c_q_g, c_k_g, c_out_g, w_out, norm2_g, w_gate, w_up, w_down):
    batch, seq, d_model = x.shape
    depth = w_in.shape[0]
    n_slots = d_model // HEAD_DIM
    a_heads = n_slots // 8
    b_heads = (n_slots - 2 * a_heads) // 2
    c_heads = n_slots - 2 * a_heads - b_heads
    a_w, b_w, c_w = a_heads * 2 * HEAD_DIM, b_heads * HEAD_DIM, c_heads * HEAD_DIM
    col_a, col_b, col_c = 0, 3 * a_w, 3 * a_w + 3 * b_w
    tn = 512
    norm_tiles = tuple((c // tn, (c + 2 * w) // tn) for c, w in ((col_a, a_w), (col_b, b_w), (col_c, c_w)))
    scale = HEAD_DIM ** -0.5
    slopes_a = _alibi_slopes(a_heads)
    slopes_c = _alibi_slopes(c_heads)

    x2 = x.reshape(batch * seq, d_model)
    for l in range(depth):
        lam_init = 0.8 - 0.6 * math.exp(-0.3 * l)
        gcols = jnp.concatenate([
            jnp.tile(a_q_g[l] * scale, 2 * a_heads), jnp.tile(a_k_g[l], 2 * a_heads), jnp.ones((a_w,), F32),
            jnp.tile(b_q_g[l] * scale, b_heads), jnp.tile(b_k_g[l], b_heads), jnp.ones((b_w,), F32),
            jnp.tile(c_q_g[l] * scale, c_heads), jnp.tile(c_k_g[l], c_heads), jnp.ones((c_w,), F32),
        ]).astype(F32).reshape(1, -1)
        lamv = jnp.stack([lambda_q1[l], lambda_k1[l], lambda_q2[l], lambda_k2[l]]).astype(F32)

        h = _rmsnorm(x2, norm1_g[l])
        proj = _inproj(h, w_in[l].astype(BF16), gcols, norm_tiles, tn=tn)
        ma = _attn_a(proj, slopes_a, lamv, a_out_g[l], batch=batch, seq=seq, heads=a_heads,
                     col0=col_a, lam_init=lam_init)
        mb = _attn_b(proj, _nbr_bias_table(b_rpb[l]), b_out_g[l], batch=batch, seq=seq,
                     heads=b_heads, col0=col_b)
        mc = _attn_c(proj, slopes_c, c_out_g[l], batch=batch, seq=seq, heads=c_heads, col0=col_c)
        x2 = _outproj(ma, mb, mc, w_out[l].astype(BF16), x2)

        d_ff = w_gate.shape[2]
        pad = (-d_ff) % 256
        wg = jnp.pad(w_gate[l].astype(BF16), ((0, 0), (0, pad)))
        wu = jnp.pad(w_up[l].astype(BF16), ((0, 0), (0, pad)))
        wd = jnp.pad(w_down[l].astype(BF16), ((0, pad), (0, 0)))
        x2 = _ffn(x2, norm2_g[l], wg, wu, wd)
    return x2.reshape(batch, seq, d_model)
```

```python
import functools
import math

import jax
import jax.numpy as jnp
from jax import lax
from jax.experimental import pallas as pl
from jax.experimental.pallas import tpu as pltpu

F32 = jnp.float32
BF16 = jnp.bfloat16

HEAD_DIM = 128
GRID_W = 64
NA_ROWS = 8
NA_COLS = 16
DILATED_BRANCHES = ((128, 1), (512, 4), (2048, 16))
RMS_EPS = 1e-6
NEG = -1e30

VMEM_LIMIT = 56 * 1024 * 1024

_NT = (((1,), (1,)), ((), ()))


def _rms(y):
    return y * lax.rsqrt(jnp.mean(y * y, axis=-1, keepdims=True) + RMS_EPS)


def _params(sem):
    return pltpu.CompilerParams(dimension_semantics=sem, vmem_limit_bytes=VMEM_LIMIT)


def _rmsnorm_kernel(x_ref, g_ref, o_ref):
    o_ref[...] = (_rms(x_ref[...]) * g_ref[...]).astype(o_ref.dtype)


def _rmsnorm(x2, g, tm=256):
    m, d = x2.shape
    return pl.pallas_call(
        _rmsnorm_kernel,
        grid=(m // tm,),
        in_specs=[pl.BlockSpec((tm, d), lambda i: (i, 0)),
                  pl.BlockSpec((1, d), lambda i: (0, 0))],
        out_specs=pl.BlockSpec((tm, d), lambda i: (i, 0)),
        out_shape=jax.ShapeDtypeStruct((m, d), BF16),
        compiler_params=_params(("arbitrary",)),
        name="rmsnorm",
    )(x2, g.reshape(1, d))


def _inproj_kernel(h_ref, w_ref, g_ref, o_ref, *, norm_tiles, tn):
    j = pl.program_id(1)
    acc = jnp.dot(h_ref[...], w_ref[0].astype(BF16), preferred_element_type=F32)
    is_norm = functools.reduce(jnp.logical_or, [(j >= lo) & (j < hi) for lo, hi in norm_tiles])

    @pl.when(is_norm)
    def _():
        for c in range(tn // HEAD_DIM):
            sl = slice(c * HEAD_DIM, (c + 1) * HEAD_DIM)
            o_ref[:, sl] = (_rms(acc[:, sl]) * g_ref[:, sl]).astype(o_ref.dtype)

    @pl.when(jnp.logical_not(is_norm))
    def _():
        o_ref[...] = acc.astype(o_ref.dtype)


def _inproj(h, w_all, layer, gcols, norm_tiles, tm=1024, tn=512):
    m, k = h.shape
    n = w_all.shape[2]
    return pl.pallas_call(
        functools.partial(_inproj_kernel, norm_tiles=norm_tiles, tn=tn),
        grid=(m // tm, n // tn),
        in_specs=[pl.BlockSpec((tm, k), lambda i, j: (i, 0)),
                  pl.BlockSpec((1, k, tn), lambda i, j: (layer, 0, j)),
                  pl.BlockSpec((1, tn), lambda i, j: (0, j))],
        out_specs=pl.BlockSpec((tm, tn), lambda i, j: (i, j)),
        out_shape=jax.ShapeDtypeStruct((m, n), BF16),
        compiler_params=_params(("arbitrary", "arbitrary")),
        name="inproj",
    )(h, w_all, gcols)


def _attn_a_kernel(slopes_ref, lamv_ref, q_ref, k_ref, v_ref, g_ref, o_ref, tb_ref,
                   *, tq, tk, seq, lam_init, group):
    h = pl.program_id(0)
    b = pl.program_id(1)
    nq = seq // tq

    @pl.when(b == 0)
    def _():
        slope = slopes_ref[h]
        d0 = (lax.broadcasted_iota(jnp.int32, (tq, tq), 1)
              - lax.broadcasted_iota(jnp.int32, (tq, tq), 0))
        for cx in range(2 * nq - 1):
            delta = d0 + (cx - (nq - 1)) * tq
            tb_ref[cx] = jnp.abs(delta).astype(F32) * (-slope)

    lv = lamv_ref[...]
    lam = (jnp.exp(jnp.sum(lv[0:1] * lv[1:2], axis=-1, keepdims=True))
           - jnp.exp(jnp.sum(lv[2:3] * lv[3:4], axis=-1, keepdims=True)) + lam_init)
    sub = tk // tq

    def one_block(qi):
        q0 = pl.multiple_of(qi * tq, tq)
        q = q_ref[pl.ds(q0, tq), :]
        m_run = [None, None]
        l_run = [None, None]
        acc = [None, None]
        for c in range(seq // tk):
            bias = jnp.concatenate([tb_ref[(c * sub + t) - qi + (nq - 1)] for t in range(sub)],
                                   axis=1)
            v = v_ref[c * tk:(c + 1) * tk, :]
            for m in range(2):
                hs = slice(m * HEAD_DIM, (m + 1) * HEAD_DIM)
                s = lax.dot_general(q[:, hs], k_ref[c * tk:(c + 1) * tk, hs], _NT,
                                    preferred_element_type=F32) + bias
                mx = jnp.max(s, axis=-1, keepdims=True)
                if c == 0:
                    m_new = mx
                    p = jnp.exp(s - m_new)
                    l_run[m] = jnp.sum(p, axis=-1, keepdims=True)
                    acc[m] = jnp.dot(p.astype(BF16), v, preferred_element_type=F32)
                else:
                    m_new = jnp.maximum(m_run[m], mx)
                    alpha = jnp.exp(m_run[m] - m_new)
                    p = jnp.exp(s - m_new)
                    l_run[m] = alpha * l_run[m] + jnp.sum(p, axis=-1, keepdims=True)
                    acc[m] = alpha * acc[m] + jnp.dot(p.astype(BF16), v,
                                                      preferred_element_type=F32)
                m_run[m] = m_new
        o = acc[0] * (1.0 / l_run[0]) - acc[1] * (lam / l_run[1])
        o_ref[pl.ds(q0, tq), :] = (_rms(o) * (g_ref[...] * (1.0 - lam_init))).astype(o_ref.dtype)

    def body(i, carry):
        for u in range(group):
            one_block(i * group + u)
        return carry

    lax.fori_loop(0, nq // group, body, 0)


def _attn_a(proj, slopes, lamv, g_out, *, batch, seq, heads, col0, lam_init, tq=256, tk=1024,
            group=2):
    vd = 2 * HEAD_DIM
    nq = seq // tq
    assert nq % group == 0 and tk % tq == 0
    qb, kb, vb = (col0 // vd, col0 // vd + heads, col0 // vd + 2 * heads)
    smem = pl.BlockSpec(memory_space=pltpu.SMEM)
    return pl.pallas_call(
        functools.partial(_attn_a_kernel, tq=tq, tk=tk, seq=seq, lam_init=lam_init, group=group),
        grid=(heads, batch),
        in_specs=[smem,
                  pl.BlockSpec((4, HEAD_DIM), lambda h, b: (0, 0)),
                  pl.BlockSpec((seq, vd), lambda h, b: (b, qb + h)),
                  pl.BlockSpec((seq, vd), lambda h, b: (b, kb + h)),
                  pl.BlockSpec((seq, vd), lambda h, b: (b, vb + h)),
                  pl.BlockSpec((1, vd), lambda h, b: (0, 0))],
        out_specs=pl.BlockSpec((seq, vd), lambda h, b: (b, h)),
        out_shape=jax.ShapeDtypeStruct((batch * seq, heads * vd), BF16),
        scratch_shapes=[pltpu.VMEM((2 * nq - 1, tq, tq), F32)],
        compiler_params=_params(("arbitrary", "arbitrary")),
        name="attn_a",
    )(slopes, lamv, proj, proj, proj, g_out.reshape(1, vd))


NB_QROWS = 4
NB_WROWS = NB_QROWS + NA_ROWS


def _nbr_plan(rows):
    variants, var_of_block = [], []
    for rb in range(rows // NB_QROWS):
        r0 = rb * NB_QROWS
        ks = min(max(r0 - NA_ROWS // 2, 0), rows - NB_WROWS)
        starts = tuple(min(max(r - NA_ROWS // 2, 0), rows - NA_ROWS) - ks
                       for r in range(r0, r0 + NB_QROWS))
        assert all(0 <= st and st + NA_ROWS <= NB_WROWS for st in starts)
        pat = (r0 - ks, starts)
        if pat not in variants:
            variants.append(pat)
        var_of_block.append(variants.index(pat))
    return variants, var_of_block


def _attn_b_kernel(q_ref, k_ref, v_ref, tab_ref, g_ref, o_ref, bias_ref, v1_ref, *, rows, group):
    b = pl.program_id(1)
    variants, var_of_block = _nbr_plan(rows)
    tq = NB_QROWS * GRID_W
    tkw = NB_WROWS * GRID_W

    @pl.when(b == 0)
    def _():
        for vi, (off, starts) in enumerate(variants):
            for qr in range(NB_QROWS):
                pieces = []
                for kr in range(NB_WROWS):
                    if starts[qr] <= kr < starts[qr] + NA_ROWS:
                        pieces.append(tab_ref[0, kr - (off + qr) + NA_ROWS - 1])
                    else:
                        pieces.append(jnp.full((GRID_W, GRID_W), NEG, F32))
                bias_ref[vi, qr * GRID_W:(qr + 1) * GRID_W, :] = jnp.concatenate(pieces, axis=1)

    v1_ref[:, :HEAD_DIM] = v_ref[...]
    v1_ref[:, HEAD_DIM:] = jnp.ones((rows * GRID_W, HEAD_DIM), v1_ref.dtype)

    default = max(set(var_of_block), key=var_of_block.count)

    def one_block(rb):
        vi = default
        for j, var in enumerate(var_of_block):
            if var != default:
                vi = jnp.where(rb == j, var, vi)
        ks = jnp.clip(rb * NB_QROWS - NA_ROWS // 2, 0, rows - NB_WROWS)
        q0 = pl.multiple_of(rb * tq, tq)
        k0 = pl.multiple_of(ks * GRID_W, GRID_W)
        s = lax.dot_general(q_ref[pl.ds(q0, tq), :], k_ref[pl.ds(k0, tkw), :], _NT,
                            preferred_element_type=F32) + bias_ref[vi]
        p = jnp.exp(s - jnp.max(s, axis=-1, keepdims=True))
        ol = jnp.dot(p.astype(BF16), v1_ref[pl.ds(k0, tkw), :], preferred_element_type=F32)
        o = ol[:, :HEAD_DIM] * (1.0 / ol[:, HEAD_DIM:HEAD_DIM + 1])
        o_ref[pl.ds(q0, tq), :] = (_rms(o) * g_ref[...]).astype(o_ref.dtype)

    def body(i, carry):
        for u in range(group):
            one_block(i * group + u)
        return carry

    lax.fori_loop(0, rows // NB_QROWS // group, body, 0)


def _attn_b(proj, tab, g_out, *, batch, seq, heads, col0, group=8):
    rows = seq // GRID_W
    assert rows % (NB_QROWS * group) == 0 and rows >= NB_WROWS
    qb = col0 // HEAD_DIM
    kb, vb = qb + heads, qb + 2 * heads
    nvar = len(_nbr_plan(rows)[0])
    return pl.pallas_call(
        functools.partial(_attn_b_kernel, rows=rows, group=group),
        grid=(heads, batch),
        in_specs=[pl.BlockSpec((seq, HEAD_DIM), lambda h, b: (b, qb + h)),
                  pl.BlockSpec((seq, HEAD_DIM), lambda h, b: (b, kb + h)),
                  pl.BlockSpec((seq, HEAD_DIM), lambda h, b: (b, vb + h)),
                  pl.BlockSpec((1, 2 * NA_ROWS - 1, GRID_W, GRID_W), lambda h, b: (h, 0, 0, 0)),
                  pl.BlockSpec((1, HEAD_DIM), lambda h, b: (0, 0))],
        out_specs=pl.BlockSpec((seq, HEAD_DIM), lambda h, b: (b, h)),
        out_shape=jax.ShapeDtypeStruct((batch * seq, heads * HEAD_DIM), BF16),
        scratch_shapes=[pltpu.VMEM((nvar, NB_QROWS * GRID_W, NB_WROWS * GRID_W), F32),
                        pltpu.VMEM((seq, 2 * HEAD_DIM), BF16)],
        compiler_params=_params(("arbitrary", "arbitrary")),
        name="attn_b",
    )(proj, proj, proj, tab, g_out.reshape(1, HEAD_DIM))


def _nbr_bias_table(rpb):
    cols = jnp.arange(GRID_W)
    c_start = jnp.clip(cols - NA_COLS // 2, 0, GRID_W - NA_COLS)
    col_mask = (cols[None, :] >= c_start[:, None]) & (cols[None, :] < c_start[:, None] + NA_COLS)
    dc = jnp.clip(cols[None, :] - cols[:, None] + NA_COLS - 1, 0, 2 * NA_COLS - 2)
    return jnp.where(col_mask[None, None], rpb[..., dc].astype(F32), NEG)


def _attn_c_kernel(slopes_ref, q_ref, k_ref, v_ref, g_ref, o_ref, dist_ref, logm_ref, tb_ref, v1_ref,
                   *, tq, seq, nwin, group):
    h = pl.program_id(0)
    b = pl.program_id(1)
    nq = seq // tq
    reach = (nwin - 1) // 2
    ncx = 2 * (nwin - 1) + 1

    @pl.when((h == 0) & (b == 0))
    def _():
        d0 = (lax.broadcasted_iota(jnp.int32, (tq, tq), 1)
              - lax.broadcasted_iota(jnp.int32, (tq, tq), 0))
        for cx in range(ncx):
            delta = d0 + (cx - (nwin - 1)) * tq
            ad = jnp.abs(delta)
            mult = jnp.zeros((tq, tq), F32)
            for window, dil in DILATED_BRANCHES:
                radius = window // (2 * dil)
                hit = ((delta & (dil - 1)) == 0) & (ad <= radius * dil)
                mult = mult + hit.astype(F32)
            dist_ref[cx] = ad.astype(F32)
            logm_ref[cx] = jnp.where(mult > 0, jnp.log(jnp.maximum(mult, 1.0)), NEG)

    @pl.when(b == 0)
    def _():
        slope = slopes_ref[h]
        for cx in range(ncx):
            tb_ref[cx] = dist_ref[cx] * (-slope) + logm_ref[cx]

    v1_ref[:, :HEAD_DIM] = v_ref[...]
    v1_ref[:, HEAD_DIM:] = jnp.ones((seq, HEAD_DIM), v1_ref.dtype)

    def one_block(qi):
        ks = jnp.clip(qi - reach, 0, nq - nwin)
        k0 = pl.multiple_of(ks * tq, tq)
        q0 = pl.multiple_of(qi * tq, tq)
        bias = jnp.concatenate([tb_ref[ks + t - qi + (nwin - 1)] for t in range(nwin)], axis=1)
        s = lax.dot_general(q_ref[pl.ds(q0, tq), :], k_ref[pl.ds(k0, nwin * tq), :], _NT,
                            preferred_element_type=F32) + bias
        p = jnp.exp(s - jnp.max(s, axis=-1, keepdims=True))
        ol = jnp.dot(p.astype(BF16), v1_ref[pl.ds(k0, nwin * tq), :], preferred_element_type=F32)
        o = ol[:, :HEAD_DIM] * (1.0 / ol[:, HEAD_DIM:HEAD_DIM + 1])
        o_ref[pl.ds(q0, tq), :] = (_rms(o) * g_ref[...]).astype(o_ref.dtype)

    def body(i, carry):
        for u in range(group):
            one_block(i * group + u)
        return carry

    lax.fori_loop(0, nq // group, body, 0)


def _attn_c(proj, slopes, g_out, *, batch, seq, heads, col0, tq=256, group=8):
    for _, dil in DILATED_BRANCHES:
        assert dil & (dil - 1) == 0
    span = max((w // (2 * d)) * d for w, d in DILATED_BRANCHES)
    nwin = 2 * (span // tq) + 1
    nq = seq // tq
    assert span % tq == 0 and nq >= nwin and nq % group == 0
    qb = col0 // HEAD_DIM
    kb, vb = qb + heads, qb + 2 * heads
    ncx = 2 * (nwin - 1) + 1
    smem = pl.BlockSpec(memory_space=pltpu.SMEM)
    return pl.pallas_call(
        functools.partial(_attn_c_kernel, tq=tq, seq=seq, nwin=nwin, group=group),
        grid=(heads, batch),
        in_specs=[smem,
                  pl.BlockSpec((seq, HEAD_DIM), lambda h, b: (b, qb + h)),
                  pl.BlockSpec((seq, HEAD_DIM), lambda h, b: (b, kb + h)),
                  pl.BlockSpec((seq, HEAD_DIM), lambda h, b: (b, vb + h)),
                  pl.BlockSpec((1, HEAD_DIM), lambda h, b: (0, 0))],
        out_specs=pl.BlockSpec((seq, HEAD_DIM), lambda h, b: (b, h)),
        out_shape=jax.ShapeDtypeStruct((batch * seq, heads * HEAD_DIM), BF16),
        scratch_shapes=[pltpu.VMEM((ncx, tq, tq), F32), pltpu.VMEM((ncx, tq, tq), F32),
                        pltpu.VMEM((ncx, tq, tq), F32), pltpu.VMEM((seq, 2 * HEAD_DIM), BF16)],
        compiler_params=_params(("arbitrary", "arbitrary")),
        name="attn_c",
    )(slopes, proj, proj, proj, g_out.reshape(1, HEAD_DIM))


def _outproj_kernel(a_ref, b_ref, c_ref, w_ref, x_ref, o_ref, *, splits):
    acc = x_ref[...]
    row = 0
    for part, width in zip((a_ref, b_ref, c_ref), splits):
        acc = acc + jnp.dot(part[...], w_ref[0, row:row + width, :].astype(BF16),
                            preferred_element_type=F32)
        row += width
    o_ref[...] = acc


def _outproj(ma, mb, mc, w_all, layer, x2, tm=1024, tn=512):
    m, n = x2.shape
    splits = (ma.shape[1], mb.shape[1], mc.shape[1])
    k = sum(splits)
    return pl.pallas_call(
        functools.partial(_outproj_kernel, splits=splits),
        grid=(m // tm, n // tn),
        in_specs=[pl.BlockSpec((tm, splits[0]), lambda i, j: (i, 0)),
                  pl.BlockSpec((tm, splits[1]), lambda i, j: (i, 0)),
                  pl.BlockSpec((tm, splits[2]), lambda i, j: (i, 0)),
                  pl.BlockSpec((1, k, tn), lambda i, j: (layer, 0, j)),
                  pl.BlockSpec((tm, tn), lambda i, j: (i, j))],
        out_specs=pl.BlockSpec((tm, tn), lambda i, j: (i, j)),
        out_shape=jax.ShapeDtypeStruct((m, n), F32),
        compiler_params=_params(("arbitrary", "arbitrary")),
        name="outproj",
    )(ma, mb, mc, w_all, x2)


def _cast_kernel(w_ref, o_ref):
    o_ref[...] = w_ref[0].astype(o_ref.dtype)


def _cast_layer(w_all, layer, tr):
    _, r, c = w_all.shape
    return pl.pallas_call(
        _cast_kernel,
        grid=(r // tr,),
        in_specs=[pl.BlockSpec((1, tr, c), lambda i: (layer, i, 0))],
        out_specs=pl.BlockSpec((tr, c), lambda i: (i, 0)),
        out_shape=jax.ShapeDtypeStruct((r, c), BF16),
        compiler_params=_params(("arbitrary",)),
        name="cast",
    )(w_all)


def _ffn_kernel(x_ref, g_ref, wg_ref, wu_ref, wd_ref, o_ref, h_ref):
    j = pl.program_id(1)

    @pl.when(j == 0)
    def _():
        x = x_ref[...]
        h_ref[...] = (_rms(x) * g_ref[...]).astype(h_ref.dtype)
        o_ref[...] = x

    h = h_ref[...]
    gate = jnp.dot(h, wg_ref[...], preferred_element_type=F32)
    up = jnp.dot(h, wu_ref[...], preferred_element_type=F32)
    act = (gate * jax.nn.sigmoid(gate) * up).astype(BF16)
    o_ref[...] += jnp.dot(act, wd_ref[...], preferred_element_type=F32)


def _ffn(x2, g, wg, wu, wd, tm=512, tf=256):
    m, d = x2.shape
    f = wg.shape[1]
    return pl.pallas_call(
        _ffn_kernel,
        grid=(m // tm, f // tf),
        in_specs=[pl.BlockSpec((tm, d), lambda i, j: (i, 0), pipeline_mode=pl.Buffered(1)),
                  pl.BlockSpec((1, d), lambda i, j: (0, 0)),
                  pl.BlockSpec((d, tf), lambda i, j: (0, j)),
                  pl.BlockSpec((d, tf), lambda i, j: (0, j)),
                  pl.BlockSpec((tf, d), lambda i, j: (j, 0))],
        out_specs=pl.BlockSpec((tm, d), lambda i, j: (i, 0)),
        out_shape=jax.ShapeDtypeStruct((m, d), F32),
        scratch_shapes=[pltpu.VMEM((tm, d), BF16)],
        compiler_params=_params(("arbitrary", "arbitrary")),
        name="ffn",
    )(x2, g.reshape(1, d), wg, wu, wd)


def _alibi_slopes(n):
    return jnp.exp2(-8.0 * jnp.arange(1, n + 1, dtype=F32) / n)


def kernel(x, norm1_g, w_in, a_q_g, a_k_g, lambda_q1, lambda_k1, lambda_q2, lambda_k2, a_out_g,
           b_q_g, b_k_g, b_rpb, b_out_g, c_q_g, c_k_g, c_out_g, w_out, norm2_g, w_gate, w_up, w_down):
    batch, seq, d_model = x.shape
    depth = w_in.shape[0]
    n_slots = d_model // HEAD_DIM
    a_heads = n_slots // 8
    b_heads = (n_slots - 2 * a_heads) // 2
    c_heads = n_slots - 2 * a_heads - b_heads
    a_w, b_w, c_w = a_heads * 2 * HEAD_DIM, b_heads * HEAD_DIM, c_heads * HEAD_DIM
    col_a, col_b, col_c = 0, 3 * a_w, 3 * a_w + 3 * b_w
    tn = 512
    norm_tiles = tuple((c // tn, (c + 2 * w) // tn) for c, w in ((col_a, a_w), (col_b, b_w), (col_c, c_w)))
    scale = HEAD_DIM ** -0.5
    slopes_a = _alibi_slopes(a_heads)
    slopes_c = _alibi_slopes(c_heads)

    x2 = x.reshape(batch * seq, d_model)
    for l in range(depth):
        lam_init = 0.8 - 0.6 * math.exp(-0.3 * l)
        gcols = jnp.concatenate([
            jnp.tile(a_q_g[l] * scale, 2 * a_heads), jnp.tile(a_k_g[l], 2 * a_heads), jnp.ones((a_w,), F32),
            jnp.tile(b_q_g[l] * scale, b_heads), jnp.tile(b_k_g[l], b_heads), jnp.ones((b_w,), F32),
            jnp.tile(c_q_g[l] * scale, c_heads), jnp.tile(c_k_g[l], c_heads), jnp.ones((c_w,), F32),
        ]).astype(F32).reshape(1, -1)
        lamv = jnp.stack([lambda_q1[l], lambda_k1[l], lambda_q2[l], lambda_k2[l]]).astype(F32)

        h = _rmsnorm(x2, norm1_g[l])
        proj = _inproj(h, w_in, l, gcols, norm_tiles, tn=tn)
        ma = _attn_a(proj, slopes_a, lamv, a_out_g[l], batch=batch, seq=seq, heads=a_heads,
                     col0=col_a, lam_init=lam_init)
        mb = _attn_b(proj, _nbr_bias_table(b_rpb[l]), b_out_g[l], batch=batch, seq=seq,
                     heads=b_heads, col0=col_b)
        mc = _attn_c(proj, slopes_c, c_out_g[l], batch=batch, seq=seq, heads=c_heads, col0=col_c)
        x2 = _outproj(ma, mb, mc, w_out, l, x2)
        x2 = _ffn(x2, norm2_g[l], _cast_layer(w_gate, l, 128), _cast_layer(w_up, l, 128),
                  _cast_layer(w_down, l, 256))
    return x2.reshape(batch, seq, d_model)
```

```python
import functools
import math

import jax
import jax.numpy as jnp
from jax import lax
from jax.experimental import pallas as pl
from jax.experimental.pallas import tpu as pltpu

F32 = jnp.float32
BF16 = jnp.bfloat16

HEAD_DIM = 128
GRID_W = 64
NA_ROWS = 8
NA_COLS = 16
DILATED_BRANCHES = ((128, 1), (512, 4), (2048, 16))
RMS_EPS = 1e-6
NEG = -1e30

VMEM_LIMIT = 56 * 1024 * 1024
FFN_VMEM_LIMIT = 60 * 1024 * 1024

_NT = (((1,), (1,)), ((), ()))


def _rms(y):
    return y * lax.rsqrt(jnp.mean(y * y, axis=-1, keepdims=True) + RMS_EPS)


def _params(sem):
    return pltpu.CompilerParams(dimension_semantics=sem, vmem_limit_bytes=VMEM_LIMIT)


def _rmsnorm_kernel(x_ref, g_ref, o_ref):
    o_ref[...] = (_rms(x_ref[...]) * g_ref[...]).astype(o_ref.dtype)


def _rmsnorm(x2, g, tm=256):
    m, d = x2.shape
    return pl.pallas_call(
        _rmsnorm_kernel,
        grid=(m // tm,),
        in_specs=[pl.BlockSpec((tm, d), lambda i: (i, 0)),
                  pl.BlockSpec((1, d), lambda i: (0, 0))],
        out_specs=pl.BlockSpec((tm, d), lambda i: (i, 0)),
        out_shape=jax.ShapeDtypeStruct((m, d), BF16),
        compiler_params=_params(("arbitrary",)),
        name="rmsnorm",
    )(x2, g.reshape(1, d))


def _inproj_kernel(h_ref, w_ref, g_ref, o_ref, *, norm_tiles, tn):
    j = pl.program_id(1)
    acc = jnp.dot(h_ref[...], w_ref[0].astype(BF16), preferred_element_type=F32)
    is_norm = functools.reduce(jnp.logical_or, [(j >= lo) & (j < hi) for lo, hi in norm_tiles])
    for c in range(tn // HEAD_DIM):
        sl = slice(c * HEAD_DIM, (c + 1) * HEAD_DIM)
        y = acc[:, sl]
        inv = lax.rsqrt(jnp.mean(y * y, axis=-1, keepdims=True) + RMS_EPS)
        o_ref[:, sl] = (y * (jnp.where(is_norm, inv, 1.0) * g_ref[:, sl])).astype(o_ref.dtype)


def _inproj(h, w_all, layer, gcols, norm_tiles, tm=1024, tn=512):
    m, k = h.shape
    n = w_all.shape[2]
    return pl.pallas_call(
        functools.partial(_inproj_kernel, norm_tiles=norm_tiles, tn=tn),
        grid=(m // tm, n // tn),
        in_specs=[pl.BlockSpec((tm, k), lambda i, j: (i, 0)),
                  pl.BlockSpec((1, k, tn), lambda i, j: (layer, 0, j)),
                  pl.BlockSpec((1, tn), lambda i, j: (0, j))],
        out_specs=pl.BlockSpec((tm, tn), lambda i, j: (i, j)),
        out_shape=jax.ShapeDtypeStruct((m, n), BF16),
        compiler_params=_params(("arbitrary", "arbitrary")),
        name="inproj",
    )(h, w_all, gcols)


def _attn_a_kernel(slopes_ref, lamv_ref, q_ref, k_ref, v_ref, g_ref, o_ref, tb_ref,
                   *, tq, tk, seq, lam_init, group):
    h = pl.program_id(0)
    b = pl.program_id(1)
    nq = seq // tq

    @pl.when(b == 0)
    def _():
        slope = slopes_ref[h]
        d0 = (lax.broadcasted_iota(jnp.int32, (tq, tq), 1)
              - lax.broadcasted_iota(jnp.int32, (tq, tq), 0))
        for cx in range(2 * nq - 1):
            delta = d0 + (cx - (nq - 1)) * tq
            tb_ref[cx] = jnp.abs(delta).astype(F32) * (-slope)

    lv = lamv_ref[...]
    lam = (jnp.exp(jnp.sum(lv[0:1] * lv[1:2], axis=-1, keepdims=True))
           - jnp.exp(jnp.sum(lv[2:3] * lv[3:4], axis=-1, keepdims=True)) + lam_init)
    sub = tk // tq

    def one_block(qi):
        q0 = pl.multiple_of(qi * tq, tq)
        q = q_ref[pl.ds(q0, tq), :]
        m_run = l_run = acc = None
        for c in range(seq // tk):
            bias = jnp.concatenate([tb_ref[(c * sub + t) - qi + (nq - 1)] for t in range(sub)],
                                   axis=1)
            s = jnp.concatenate(
                [lax.dot_general(q[:, m * HEAD_DIM:(m + 1) * HEAD_DIM],
                                 k_ref[c * tk:(c + 1) * tk, m * HEAD_DIM:(m + 1) * HEAD_DIM], _NT,
                                 preferred_element_type=F32) + bias for m in range(2)], axis=0)
            mx = jnp.max(s, axis=-1, keepdims=True)
            m_new = mx if c == 0 else jnp.maximum(m_run, mx)
            p = jnp.exp(s - m_new)
            l_new = jnp.sum(p, axis=-1, keepdims=True)
            pv = jnp.dot(p.astype(BF16), v_ref[c * tk:(c + 1) * tk, :], preferred_element_type=F32)
            if c == 0:
                l_run, acc = l_new, pv
            else:
                alpha = jnp.exp(m_run - m_new)
                l_run = alpha * l_run + l_new
                acc = alpha * acc + pv
            m_run = m_new
        o = acc[:tq] * (1.0 / l_run[:tq]) - acc[tq:] * (lam / l_run[tq:])
        o_ref[pl.ds(q0, tq), :] = (_rms(o) * (g_ref[...] * (1.0 - lam_init))).astype(o_ref.dtype)

    def body(i, carry):
        for u in range(group):
            one_block(i * group + u)
        return carry

    lax.fori_loop(0, nq // group, body, 0)


def _attn_a(proj, slopes, lamv, g_out, *, batch, seq, heads, col0, lam_init, tq=256, tk=1024,
            group=2):
    vd = 2 * HEAD_DIM
    nq = seq // tq
    assert nq % group == 0 and tk % tq == 0
    qb, kb, vb = (col0 // vd, col0 // vd + heads, col0 // vd + 2 * heads)
    smem = pl.BlockSpec(memory_space=pltpu.SMEM)
    return pl.pallas_call(
        functools.partial(_attn_a_kernel, tq=tq, tk=tk, seq=seq, lam_init=lam_init, group=group),
        grid=(heads, batch),
        in_specs=[smem,
                  pl.BlockSpec((4, HEAD_DIM), lambda h, b: (0, 0)),
                  pl.BlockSpec((seq, vd), lambda h, b: (b, qb + h)),
                  pl.BlockSpec((seq, vd), lambda h, b: (b, kb + h)),
                  pl.BlockSpec((seq, vd), lambda h, b: (b, vb + h)),
                  pl.BlockSpec((1, vd), lambda h, b: (0, 0))],
        out_specs=pl.BlockSpec((seq, vd), lambda h, b: (b, h)),
        out_shape=jax.ShapeDtypeStruct((batch * seq, heads * vd), BF16),
        scratch_shapes=[pltpu.VMEM((2 * nq - 1, tq, tq), F32)],
        compiler_params=_params(("arbitrary", "arbitrary")),
        name="attn_a",
    )(slopes, lamv, proj, proj, proj, g_out.reshape(1, vd))


NB_QROWS = 4
NB_WROWS = NB_QROWS + NA_ROWS


def _nbr_plan(rows):
    variants, var_of_block = [], []
    for rb in range(rows // NB_QROWS):
        r0 = rb * NB_QROWS
        ks = min(max(r0 - NA_ROWS // 2, 0), rows - NB_WROWS)
        starts = tuple(min(max(r - NA_ROWS // 2, 0), rows - NA_ROWS) - ks
                       for r in range(r0, r0 + NB_QROWS))
        assert all(0 <= st and st + NA_ROWS <= NB_WROWS for st in starts)
        pat = (r0 - ks, starts)
        if pat not in variants:
            variants.append(pat)
        var_of_block.append(variants.index(pat))
    return variants, var_of_block


def _attn_b_kernel(q_ref, k_ref, v_ref, tab_ref, g_ref, o_ref, bias_ref, v1_ref, *, rows, group):
    b = pl.program_id(1)
    variants, var_of_block = _nbr_plan(rows)
    tq = NB_QROWS * GRID_W
    tkw = NB_WROWS * GRID_W

    @pl.when(b == 0)
    def _():
        for vi, (off, starts) in enumerate(variants):
            for qr in range(NB_QROWS):
                pieces = []
                for kr in range(NB_WROWS):
                    if starts[qr] <= kr < starts[qr] + NA_ROWS:
                        pieces.append(tab_ref[0, kr - (off + qr) + NA_ROWS - 1])
                    else:
                        pieces.append(jnp.full((GRID_W, GRID_W), NEG, F32))
                bias_ref[vi, qr * GRID_W:(qr + 1) * GRID_W, :] = jnp.concatenate(pieces, axis=1)

    v1_ref[:, :HEAD_DIM] = v_ref[...]
    v1_ref[:, HEAD_DIM:] = jnp.ones((rows * GRID_W, HEAD_DIM), v1_ref.dtype)

    default = max(set(var_of_block), key=var_of_block.count)

    def one_block(rb):
        vi = default
        for j, var in enumerate(var_of_block):
            if var != default:
                vi = jnp.where(rb == j, var, vi)
        ks = jnp.clip(rb * NB_QROWS - NA_ROWS // 2, 0, rows - NB_WROWS)
        q0 = pl.multiple_of(rb * tq, tq)
        k0 = pl.multiple_of(ks * GRID_W, GRID_W)
        s = lax.dot_general(q_ref[pl.ds(q0, tq), :], k_ref[pl.ds(k0, tkw), :], _NT,
                            preferred_element_type=F32) + bias_ref[vi]
        p = jnp.exp(s - jnp.max(s, axis=-1, keepdims=True))
        ol = jnp.dot(p.astype(BF16), v1_ref[pl.ds(k0, tkw), :], preferred_element_type=F32)
        o = ol[:, :HEAD_DIM] * (1.0 / ol[:, HEAD_DIM:HEAD_DIM + 1])
        o_ref[pl.ds(q0, tq), :] = (_rms(o) * g_ref[...]).astype(o_ref.dtype)

    def body(i, carry):
        for u in range(group):
            one_block(i * group + u)
        return carry

    lax.fori_loop(0, rows // NB_QROWS // group, body, 0)


def _attn_b(proj, tab, g_out, *, batch, seq, heads, col0, group=8):
    rows = seq // GRID_W
    assert rows % (NB_QROWS * group) == 0 and rows >= NB_WROWS
    qb = col0 // HEAD_DIM
    kb, vb = qb + heads, qb + 2 * heads
    nvar = len(_nbr_plan(rows)[0])
    return pl.pallas_call(
        functools.partial(_attn_b_kernel, rows=rows, group=group),
        grid=(heads, batch),
        in_specs=[pl.BlockSpec((seq, HEAD_DIM), lambda h, b: (b, qb + h)),
                  pl.BlockSpec((seq, HEAD_DIM), lambda h, b: (b, kb + h)),
                  pl.BlockSpec((seq, HEAD_DIM), lambda h, b: (b, vb + h)),
                  pl.BlockSpec((1, 2 * NA_ROWS - 1, GRID_W, GRID_W), lambda h, b: (h, 0, 0, 0)),
                  pl.BlockSpec((1, HEAD_DIM), lambda h, b: (0, 0))],
        out_specs=pl.BlockSpec((seq, HEAD_DIM), lambda h, b: (b, h)),
        out_shape=jax.ShapeDtypeStruct((batch * seq, heads * HEAD_DIM), BF16),
        scratch_shapes=[pltpu.VMEM((nvar, NB_QROWS * GRID_W, NB_WROWS * GRID_W), F32),
                        pltpu.VMEM((seq, 2 * HEAD_DIM), BF16)],
        compiler_params=_params(("arbitrary", "arbitrary")),
        name="attn_b",
    )(proj, proj, proj, tab, g_out.reshape(1, HEAD_DIM))


def _nbr_bias_table(rpb):
    cols = jnp.arange(GRID_W)
    c_start = jnp.clip(cols - NA_COLS // 2, 0, GRID_W - NA_COLS)
    col_mask = (cols[None, :] >= c_start[:, None]) & (cols[None, :] < c_start[:, None] + NA_COLS)
    dc = cols[None, :] - cols[:, None] + NA_COLS - 1
    t = jnp.full(rpb.shape[:2] + (GRID_W, GRID_W), NEG, F32)
    for kc in range(2 * NA_COLS - 1):
        t = jnp.where((dc == kc) & col_mask, rpb[:, :, kc, None, None].astype(F32), t)
    return t


def _attn_c_kernel(slopes_ref, q_ref, k_ref, v_ref, g_ref, o_ref, dist_ref, logm_ref, tb_ref, v1_ref,
                   *, tq, seq, nwin, group):
    h = pl.program_id(0)
    b = pl.program_id(1)
    nq = seq // tq
    reach = (nwin - 1) // 2
    ncx = 2 * (nwin - 1) + 1

    @pl.when((h == 0) & (b == 0))
    def _():
        d0 = (lax.broadcasted_iota(jnp.int32, (tq, tq), 1)
              - lax.broadcasted_iota(jnp.int32, (tq, tq), 0))
        for cx in range(ncx):
            delta = d0 + (cx - (nwin - 1)) * tq
            ad = jnp.abs(delta)
            mult = jnp.zeros((tq, tq), F32)
            for window, dil in DILATED_BRANCHES:
                radius = window // (2 * dil)
                hit = ((delta & (dil - 1)) == 0) & (ad <= radius * dil)
                mult = mult + hit.astype(F32)
            dist_ref[cx] = ad.astype(F32)
            logm_ref[cx] = jnp.where(mult > 0, jnp.log(jnp.maximum(mult, 1.0)), NEG)

    @pl.when(b == 0)
    def _():
        slope = slopes_ref[h]
        for cx in range(ncx):
            tb_ref[cx] = dist_ref[cx] * (-slope) + logm_ref[cx]

    v1_ref[:, :HEAD_DIM] = v_ref[...]
    v1_ref[:, HEAD_DIM:] = jnp.ones((seq, HEAD_DIM), v1_ref.dtype)

    def one_block(qi):
        ks = jnp.clip(qi - reach, 0, nq - nwin)
        k0 = pl.multiple_of(ks * tq, tq)
        q0 = pl.multiple_of(qi * tq, tq)
        bias = jnp.concatenate([tb_ref[ks + t - qi + (nwin - 1)] for t in range(nwin)], axis=1)
        s = lax.dot_general(q_ref[pl.ds(q0, tq), :], k_ref[pl.ds(k0, nwin * tq), :], _NT,
                            preferred_element_type=F32) + bias
        p = jnp.exp(s - jnp.max(s, axis=-1, keepdims=True))
        ol = jnp.dot(p.astype(BF16), v1_ref[pl.ds(k0, nwin * tq), :], preferred_element_type=F32)
        o = ol[:, :HEAD_DIM] * (1.0 / ol[:, HEAD_DIM:HEAD_DIM + 1])
        o_ref[pl.ds(q0, tq), :] = (_rms(o) * g_ref[...]).astype(o_ref.dtype)

    def body(i, carry):
        for u in range(group):
            one_block(i * group + u)
        return carry

    lax.fori_loop(0, nq // group, body, 0)


def _attn_c(proj, slopes, g_out, *, batch, seq, heads, col0, tq=256, group=8):
    for _, dil in DILATED_BRANCHES:
        assert dil & (dil - 1) == 0
    span = max((w // (2 * d)) * d for w, d in DILATED_BRANCHES)
    nwin = 2 * (span // tq) + 1
    nq = seq // tq
    assert span % tq == 0 and nq >= nwin and nq % group == 0
    qb = col0 // HEAD_DIM
    kb, vb = qb + heads, qb + 2 * heads
    ncx = 2 * (nwin - 1) + 1
    smem = pl.BlockSpec(memory_space=pltpu.SMEM)
    return pl.pallas_call(
        functools.partial(_attn_c_kernel, tq=tq, seq=seq, nwin=nwin, group=group),
        grid=(heads, batch),
        in_specs=[smem,
                  pl.BlockSpec((seq, HEAD_DIM), lambda h, b: (b, qb + h)),
                  pl.BlockSpec((seq, HEAD_DIM), lambda h, b: (b, kb + h)),
                  pl.BlockSpec((seq, HEAD_DIM), lambda h, b: (b, vb + h)),
                  pl.BlockSpec((1, HEAD_DIM), lambda h, b: (0, 0))],
        out_specs=pl.BlockSpec((seq, HEAD_DIM), lambda h, b: (b, h)),
        out_shape=jax.ShapeDtypeStruct((batch * seq, heads * HEAD_DIM), BF16),
        scratch_shapes=[pltpu.VMEM((ncx, tq, tq), F32), pltpu.VMEM((ncx, tq, tq), F32),
                        pltpu.VMEM((ncx, tq, tq), F32), pltpu.VMEM((seq, 2 * HEAD_DIM), BF16)],
        compiler_params=_params(("arbitrary", "arbitrary")),
        name="attn_c",
    )(slopes, proj, proj, proj, g_out.reshape(1, HEAD_DIM))


def _outproj_kernel(a_ref, b_ref, c_ref, w_ref, x_ref, o_ref, *, splits):
    acc = x_ref[...]
    row = 0
    for part, width in zip((a_ref, b_ref, c_ref), splits):
        acc = acc + jnp.dot(part[...], w_ref[0, row:row + width, :].astype(BF16),
                            preferred_element_type=F32)
        row += width
    o_ref[...] = acc


def _outproj(ma, mb, mc, w_all, layer, x2, tm=1024, tn=512):
    m, n = x2.shape
    splits = (ma.shape[1], mb.shape[1], mc.shape[1])
    k = sum(splits)
    return pl.pallas_call(
        functools.partial(_outproj_kernel, splits=splits),
        grid=(m // tm, n // tn),
        in_specs=[pl.BlockSpec((tm, splits[0]), lambda i, j: (i, 0)),
                  pl.BlockSpec((tm, splits[1]), lambda i, j: (i, 0)),
                  pl.BlockSpec((tm, splits[2]), lambda i, j: (i, 0)),
                  pl.BlockSpec((1, k, tn), lambda i, j: (layer, 0, j)),
                  pl.BlockSpec((tm, tn), lambda i, j: (i, j))],
        out_specs=pl.BlockSpec((tm, tn), lambda i, j: (i, j)),
        out_shape=jax.ShapeDtypeStruct((m, n), F32),
        compiler_params=_params(("arbitrary", "arbitrary")),
        name="outproj",
    )(ma, mb, mc, w_all, x2)


def _cast_kernel(w_ref, o_ref):
    o_ref[...] = w_ref[0].astype(o_ref.dtype)


def _cast_layer(w_all, layer, tr):
    _, r, c = w_all.shape
    return pl.pallas_call(
        _cast_kernel,
        grid=(r // tr,),
        in_specs=[pl.BlockSpec((1, tr, c), lambda i: (layer, i, 0))],
        out_specs=pl.BlockSpec((tr, c), lambda i: (i, 0)),
        out_shape=jax.ShapeDtypeStruct((r, c), BF16),
        compiler_params=_params(("arbitrary",)),
        name="cast",
    )(w_all)


def _ffn_kernel(x_hbm, g_ref, wg_ref, wu_ref, wd_ref, o_ref, h_ref, sem, *, tm, norm_rows):
    i = pl.program_id(0)
    j = pl.program_id(1)

    @pl.when(j == 0)
    def _():
        load_x = pltpu.make_async_copy(x_hbm.at[pl.ds(pl.multiple_of(i * tm, tm), tm), :], o_ref, sem)
        load_x.start()
        load_x.wait()

        def norm(r, carry):
            rows = pl.ds(pl.multiple_of(r * norm_rows, norm_rows), norm_rows)
            h_ref[rows, :] = (_rms(o_ref[rows, :]) * g_ref[...]).astype(h_ref.dtype)
            return carry

        lax.fori_loop(0, tm // norm_rows, norm, 0)

    h = h_ref[...]
    gate = jnp.dot(h, wg_ref[...], preferred_element_type=F32)
    up = jnp.dot(h, wu_ref[...], preferred_element_type=F32)
    act = (gate * jax.nn.sigmoid(gate) * up).astype(BF16)
    o_ref[...] += jnp.dot(act, wd_ref[...], preferred_element_type=F32)


def _ffn(x2, g, wg, wu, wd, tm=1024, tf=256, norm_rows=128):
    m, d = x2.shape
    f = wg.shape[1]
    return pl.pallas_call(
        functools.partial(_ffn_kernel, tm=tm, norm_rows=norm_rows),
        grid=(m // tm, f // tf),
        in_specs=[pl.BlockSpec(memory_space=pl.ANY),
                  pl.BlockSpec((1, d), lambda i, j: (0, 0)),
                  pl.BlockSpec((d, tf), lambda i, j: (0, j)),
                  pl.BlockSpec((d, tf), lambda i, j: (0, j)),
                  pl.BlockSpec((tf, d), lambda i, j: (j, 0))],
        out_specs=pl.BlockSpec((tm, d), lambda i, j: (i, 0)),
        out_shape=jax.ShapeDtypeStruct((m, d), F32),
        scratch_shapes=[pltpu.VMEM((tm, d), BF16), pltpu.SemaphoreType.DMA(())],
        compiler_params=pltpu.CompilerParams(dimension_semantics=("arbitrary", "arbitrary"),
                                             vmem_limit_bytes=FFN_VMEM_LIMIT),
        name="ffn",
    )(x2, g.reshape(1, d), wg, wu, wd)


def _alibi_slopes(n):
    return jnp.exp2(-8.0 * jnp.arange(1, n + 1, dtype=F32) / n)


def kernel(x, norm1_g, w_in, a_q_g, a_k_g, lambda_q1, lambda_k1, lambda_q2, lambda_k2, a_out_g,
           b_q_g, b_k_g, b_rpb, b_out_g, c_q_g, c_k_g, c_out_g, w_out, norm2_g, w_gate, w_up, w_down):
    batch, seq, d_model = x.shape
    depth = w_in.shape[0]
    n_slots = d_model // HEAD_DIM
    a_heads = n_slots // 8
    b_heads = (n_slots - 2 * a_heads) // 2
    c_heads = n_slots - 2 * a_heads - b_heads
    a_w, b_w, c_w = a_heads * 2 * HEAD_DIM, b_heads * HEAD_DIM, c_heads * HEAD_DIM
    col_a, col_b, col_c = 0, 3 * a_w, 3 * a_w + 3 * b_w
    tn = 512
    norm_tiles = tuple((c // tn, (c + 2 * w) // tn) for c, w in ((col_a, a_w), (col_b, b_w), (col_c, c_w)))
    scale = HEAD_DIM ** -0.5
    slopes_a = _alibi_slopes(a_heads)
    slopes_c = _alibi_slopes(c_heads)

    x2 = x.reshape(batch * seq, d_model)
    for l in range(depth):
        lam_init = 0.8 - 0.6 * math.exp(-0.3 * l)
        gcols = jnp.concatenate([
            jnp.tile(a_q_g[l] * scale, 2 * a_heads), jnp.tile(a_k_g[l], 2 * a_heads), jnp.ones((a_w,), F32),
            jnp.tile(b_q_g[l] * scale, b_heads), jnp.tile(b_k_g[l], b_heads), jnp.ones((b_w,), F32),
            jnp.tile(c_q_g[l] * scale, c_heads), jnp.tile(c_k_g[l], c_heads), jnp.ones((c_w,), F32),
        ]).astype(F32).reshape(1, -1)
        lamv = jnp.stack([lambda_q1[l], lambda_k1[l], lambda_q2[l], lambda_k2[l]]).astype(F32)

        h = _rmsnorm(x2, norm1_g[l])
        proj = _inproj(h, w_in, l, gcols, norm_tiles, tn=tn)
        ma = _attn_a(proj, slopes_a, lamv, a_out_g[l], batch=batch, seq=seq, heads=a_heads,
                     col0=col_a, lam_init=lam_init)
        mb = _attn_b(proj, _nbr_bias_table(b_rpb[l]), b_out_g[l], batch=batch, seq=seq,
                     heads=b_heads, col0=col_b)
        mc = _attn_c(proj, slopes_c, c_out_g[l], batch=batch, seq=seq, heads=c_heads, col0=col_c)
        x2 = _outproj(ma, mb, mc, w_out, l, x2)
        x2 = _ffn(x2, norm2_g[l], _cast_layer(w_gate, l, 128), _cast_layer(w_up, l, 128),
                  _cast_layer(w_down, l, 256))
    return x2.reshape(batch, seq, d_model)
```

```python
import functools
import math

import jax
import jax.numpy as jnp
from jax import lax
from jax.experimental import pallas as pl
from jax.experimental.pallas import tpu as pltpu

F32 = jnp.float32
BF16 = jnp.bfloat16

HEAD_DIM = 128
GRID_W = 64
NA_ROWS = 8
NA_COLS = 16
DILATED_BRANCHES = ((128, 1), (512, 4), (2048, 16))
RMS_EPS = 1e-6
NEG = -1e30

VMEM_LIMIT = 56 * 1024 * 1024
FFN_VMEM_LIMIT = 60 * 1024 * 1024

_NT = (((1,), (1,)), ((), ()))


def _rms(y):
    return y * lax.rsqrt(jnp.mean(y * y, axis=-1, keepdims=True) + RMS_EPS)


def _params(sem):
    return pltpu.CompilerParams(dimension_semantics=sem, vmem_limit_bytes=VMEM_LIMIT)


def _rmsnorm_kernel(x_ref, g_ref, o_ref):
    o_ref[...] = (_rms(x_ref[...]) * g_ref[...]).astype(o_ref.dtype)


def _rmsnorm(x2, g, tm=256):
    m, d = x2.shape
    return pl.pallas_call(
        _rmsnorm_kernel,
        grid=(m // tm,),
        in_specs=[pl.BlockSpec((tm, d), lambda i: (i, 0)),
                  pl.BlockSpec((1, d), lambda i: (0, 0))],
        out_specs=pl.BlockSpec((tm, d), lambda i: (i, 0)),
        out_shape=jax.ShapeDtypeStruct((m, d), BF16),
        compiler_params=_params(("arbitrary",)),
        name="rmsnorm",
    )(x2, g.reshape(1, d))


def _inproj_kernel(h_ref, w_ref, g_ref, o_ref, *, norm_tiles, tn):
    j = pl.program_id(1)
    acc = jnp.dot(h_ref[...], w_ref[0].astype(BF16), preferred_element_type=F32)
    is_norm = functools.reduce(jnp.logical_or, [(j >= lo) & (j < hi) for lo, hi in norm_tiles])
    for c in range(tn // HEAD_DIM):
        sl = slice(c * HEAD_DIM, (c + 1) * HEAD_DIM)
        y = acc[:, sl]
        inv = lax.rsqrt(jnp.mean(y * y, axis=-1, keepdims=True) + RMS_EPS)
        o_ref[:, sl] = (y * (jnp.where(is_norm, inv, 1.0) * g_ref[:, sl])).astype(o_ref.dtype)


def _inproj(h, w_all, layer, gcols, norm_tiles, tm=1024, tn=512):
    m, k = h.shape
    n = w_all.shape[2]
    return pl.pallas_call(
        functools.partial(_inproj_kernel, norm_tiles=norm_tiles, tn=tn),
        grid=(m // tm, n // tn),
        in_specs=[pl.BlockSpec((tm, k), lambda i, j: (i, 0)),
                  pl.BlockSpec((1, k, tn), lambda i, j: (layer, 0, j)),
                  pl.BlockSpec((1, tn), lambda i, j: (0, j))],
        out_specs=pl.BlockSpec((tm, tn), lambda i, j: (i, j)),
        out_shape=jax.ShapeDtypeStruct((m, n), BF16),
        compiler_params=_params(("arbitrary", "arbitrary")),
        name="inproj",
    )(h, w_all, gcols)


def _attn_a_kernel(slopes_ref, lamv_ref, q_ref, k_ref, v_ref, g_ref, o_ref, tb_ref,
                   *, tq, tk, seq, lam_init, group):
    h = pl.program_id(0)
    b = pl.program_id(1)
    nq = seq // tq

    @pl.when(b == 0)
    def _():
        slope = slopes_ref[h]
        d0 = (lax.broadcasted_iota(jnp.int32, (tq, tq), 1)
              - lax.broadcasted_iota(jnp.int32, (tq, tq), 0))
        for cx in range(2 * nq - 1):
            delta = d0 + (cx - (nq - 1)) * tq
            tb_ref[cx] = jnp.abs(delta).astype(F32) * (-slope)

    lv = lamv_ref[...]
    lam = (jnp.exp(jnp.sum(lv[0:1] * lv[1:2], axis=-1, keepdims=True))
           - jnp.exp(jnp.sum(lv[2:3] * lv[3:4], axis=-1, keepdims=True)) + lam_init)
    sub = tk // tq

    def one_block(qi):
        q0 = pl.multiple_of(qi * tq, tq)
        q = q_ref[pl.ds(q0, tq), :]
        m_run = l_run = acc = None
        for c in range(seq // tk):
            bias = jnp.concatenate([tb_ref[(c * sub + t) - qi + (nq - 1)] for t in range(sub)],
                                   axis=1)
            s = jnp.concatenate(
                [lax.dot_general(q[:, m * HEAD_DIM:(m + 1) * HEAD_DIM],
                                 k_ref[c * tk:(c + 1) * tk, m * HEAD_DIM:(m + 1) * HEAD_DIM], _NT,
                                 preferred_element_type=F32) + bias for m in range(2)], axis=0)
            mx = jnp.max(s, axis=-1, keepdims=True)
            m_new = mx if c == 0 else jnp.maximum(m_run, mx)
            p = jnp.exp(s - m_new)
            l_new = jnp.sum(p, axis=-1, keepdims=True)
            pv = jnp.dot(p.astype(BF16), v_ref[c * tk:(c + 1) * tk, :], preferred_element_type=F32)
            if c == 0:
                l_run, acc = l_new, pv
            else:
                alpha = jnp.exp(m_run - m_new)
                l_run = alpha * l_run + l_new
                acc = alpha * acc + pv
            m_run = m_new
        o = acc[:tq] * (1.0 / l_run[:tq]) - acc[tq:] * (lam / l_run[tq:])
        o_ref[pl.ds(q0, tq), :] = (_rms(o) * (g_ref[...] * (1.0 - lam_init))).astype(o_ref.dtype)

    def body(i, carry):
        for u in range(group):
            one_block(i * group + u)
        return carry

    lax.fori_loop(0, nq // group, body, 0)


def _attn_a(proj, slopes, lamv, g_out, *, batch, seq, heads, col0, lam_init, tq=256, tk=1024,
            group=2):
    vd = 2 * HEAD_DIM
    nq = seq // tq
    assert nq % group == 0 and tk % tq == 0
    qb, kb, vb = (col0 // vd, col0 // vd + heads, col0 // vd + 2 * heads)
    smem = pl.BlockSpec(memory_space=pltpu.SMEM)
    return pl.pallas_call(
        functools.partial(_attn_a_kernel, tq=tq, tk=tk, seq=seq, lam_init=lam_init, group=group),
        grid=(heads, batch),
        in_specs=[smem,
                  pl.BlockSpec((4, HEAD_DIM), lambda h, b: (0, 0)),
                  pl.BlockSpec((seq, vd), lambda h, b: (b, qb + h)),
                  pl.BlockSpec((seq, vd), lambda h, b: (b, kb + h)),
                  pl.BlockSpec((seq, vd), lambda h, b: (b, vb + h)),
                  pl.BlockSpec((1, vd), lambda h, b: (0, 0))],
        out_specs=pl.BlockSpec((seq, vd), lambda h, b: (b, h)),
        out_shape=jax.ShapeDtypeStruct((batch * seq, heads * vd), BF16),
        scratch_shapes=[pltpu.VMEM((2 * nq - 1, tq, tq), F32)],
        compiler_params=_params(("arbitrary", "arbitrary")),
        name="attn_a",
    )(slopes, lamv, proj, proj, proj, g_out.reshape(1, vd))


NB_QROWS = 4
NB_WROWS = NB_QROWS + NA_ROWS


def _nbr_plan(rows):
    variants, var_of_block = [], []
    for rb in range(rows // NB_QROWS):
        r0 = rb * NB_QROWS
        ks = min(max(r0 - NA_ROWS // 2, 0), rows - NB_WROWS)
        starts = tuple(min(max(r - NA_ROWS // 2, 0), rows - NA_ROWS) - ks
                       for r in range(r0, r0 + NB_QROWS))
        assert all(0 <= st and st + NA_ROWS <= NB_WROWS for st in starts)
        pat = (r0 - ks, starts)
        if pat not in variants:
            variants.append(pat)
        var_of_block.append(variants.index(pat))
    return variants, var_of_block


def _attn_b_kernel(q_ref, k_ref, v_ref, tab_ref, g_ref, o_ref, bias_ref, v1_ref, *, rows, group):
    b = pl.program_id(1)
    variants, var_of_block = _nbr_plan(rows)
    tq = NB_QROWS * GRID_W
    tkw = NB_WROWS * GRID_W

    @pl.when(b == 0)
    def _():
        for vi, (off, starts) in enumerate(variants):
            for qr in range(NB_QROWS):
                pieces = []
                for kr in range(NB_WROWS):
                    if starts[qr] <= kr < starts[qr] + NA_ROWS:
                        pieces.append(tab_ref[0, kr - (off + qr) + NA_ROWS - 1])
                    else:
                        pieces.append(jnp.full((GRID_W, GRID_W), NEG, F32))
                bias_ref[vi, qr * GRID_W:(qr + 1) * GRID_W, :] = jnp.concatenate(pieces, axis=1)

    v1_ref[:, :HEAD_DIM] = v_ref[...]
    v1_ref[:, HEAD_DIM:] = jnp.ones((rows * GRID_W, HEAD_DIM), v1_ref.dtype)

    default = max(set(var_of_block), key=var_of_block.count)

    def one_block(rb):
        vi = default
        for j, var in enumerate(var_of_block):
            if var != default:
                vi = jnp.where(rb == j, var, vi)
        ks = jnp.clip(rb * NB_QROWS - NA_ROWS // 2, 0, rows - NB_WROWS)
        q0 = pl.multiple_of(rb * tq, tq)
        k0 = pl.multiple_of(ks * GRID_W, GRID_W)
        s = lax.dot_general(q_ref[pl.ds(q0, tq), :], k_ref[pl.ds(k0, tkw), :], _NT,
                            preferred_element_type=F32) + bias_ref[vi]
        p = jnp.exp(s - jnp.max(s, axis=-1, keepdims=True))
        ol = jnp.dot(p.astype(BF16), v1_ref[pl.ds(k0, tkw), :], preferred_element_type=F32)
        o = ol[:, :HEAD_DIM] * (1.0 / ol[:, HEAD_DIM:HEAD_DIM + 1])
        o_ref[pl.ds(q0, tq), :] = (_rms(o) * g_ref[...]).astype(o_ref.dtype)

    def body(i, carry):
        for u in range(group):
            one_block(i * group + u)
        return carry

    lax.fori_loop(0, rows // NB_QROWS // group, body, 0)


def _attn_b(proj, tab, g_out, *, batch, seq, heads, col0, group=8):
    rows = seq // GRID_W
    assert rows % (NB_QROWS * group) == 0 and rows >= NB_WROWS
    qb = col0 // HEAD_DIM
    kb, vb = qb + heads, qb + 2 * heads
    nvar = len(_nbr_plan(rows)[0])
    return pl.pallas_call(
        functools.partial(_attn_b_kernel, rows=rows, group=group),
        grid=(heads, batch),
        in_specs=[pl.BlockSpec((seq, HEAD_DIM), lambda h, b: (b, qb + h)),
                  pl.BlockSpec((seq, HEAD_DIM), lambda h, b: (b, kb + h)),
                  pl.BlockSpec((seq, HEAD_DIM), lambda h, b: (b, vb + h)),
                  pl.BlockSpec((1, 2 * NA_ROWS - 1, GRID_W, GRID_W), lambda h, b: (h, 0, 0, 0)),
                  pl.BlockSpec((1, HEAD_DIM), lambda h, b: (0, 0))],
        out_specs=pl.BlockSpec((seq, HEAD_DIM), lambda h, b: (b, h)),
        out_shape=jax.ShapeDtypeStruct((batch * seq, heads * HEAD_DIM), BF16),
        scratch_shapes=[pltpu.VMEM((nvar, NB_QROWS * GRID_W, NB_WROWS * GRID_W), F32),
                        pltpu.VMEM((seq, 2 * HEAD_DIM), BF16)],
        compiler_params=_params(("arbitrary", "arbitrary")),
        name="attn_b",
    )(proj, proj, proj, tab, g_out.reshape(1, HEAD_DIM))


def _nbr_bias_table(rpb):
    cols = jnp.arange(GRID_W)
    c_start = jnp.clip(cols - NA_COLS // 2, 0, GRID_W - NA_COLS)
    col_mask = (cols[None, :] >= c_start[:, None]) & (cols[None, :] < c_start[:, None] + NA_COLS)
    dc = cols[None, :] - cols[:, None] + NA_COLS - 1
    t = jnp.full(rpb.shape[:2] + (GRID_W, GRID_W), NEG, F32)
    for kc in range(2 * NA_COLS - 1):
        t = jnp.where((dc == kc) & col_mask, rpb[:, :, kc, None, None].astype(F32), t)
    return t


def _attn_c_kernel(slopes_ref, q_ref, k_ref, v_ref, g_ref, o_ref, dist_ref, logm_ref, tb_ref, v1_ref,
                   *, tq, seq, nwin, group):
    h = pl.program_id(0)
    b = pl.program_id(1)
    nq = seq // tq
    reach = (nwin - 1) // 2
    ncx = 2 * (nwin - 1) + 1

    @pl.when((h == 0) & (b == 0))
    def _():
        d0 = (lax.broadcasted_iota(jnp.int32, (tq, tq), 1)
              - lax.broadcasted_iota(jnp.int32, (tq, tq), 0))
        for cx in range(ncx):
            delta = d0 + (cx - (nwin - 1)) * tq
            ad = jnp.abs(delta)
            mult = jnp.zeros((tq, tq), F32)
            for window, dil in DILATED_BRANCHES:
                radius = window // (2 * dil)
                hit = ((delta & (dil - 1)) == 0) & (ad <= radius * dil)
                mult = mult + hit.astype(F32)
            dist_ref[cx] = ad.astype(F32)
            logm_ref[cx] = jnp.where(mult > 0, jnp.log(jnp.maximum(mult, 1.0)), NEG)

    @pl.when(b == 0)
    def _():
        slope = slopes_ref[h]
        for cx in range(ncx):
            tb_ref[cx] = dist_ref[cx] * (-slope) + logm_ref[cx]

    v1_ref[:, :HEAD_DIM] = v_ref[...]
    v1_ref[:, HEAD_DIM:] = jnp.ones((seq, HEAD_DIM), v1_ref.dtype)

    def one_block(qi):
        ks = jnp.clip(qi - reach, 0, nq - nwin)
        k0 = pl.multiple_of(ks * tq, tq)
        q0 = pl.multiple_of(qi * tq, tq)
        bias = jnp.concatenate([tb_ref[ks + t - qi + (nwin - 1)] for t in range(nwin)], axis=1)
        s = lax.dot_general(q_ref[pl.ds(q0, tq), :], k_ref[pl.ds(k0, nwin * tq), :], _NT,
                            preferred_element_type=F32) + bias
        p = jnp.exp(s - jnp.max(s, axis=-1, keepdims=True))
        ol = jnp.dot(p.astype(BF16), v1_ref[pl.ds(k0, nwin * tq), :], preferred_element_type=F32)
        o = ol[:, :HEAD_DIM] * (1.0 / ol[:, HEAD_DIM:HEAD_DIM + 1])
        o_ref[pl.ds(q0, tq), :] = (_rms(o) * g_ref[...]).astype(o_ref.dtype)

    def body(i, carry):
        for u in range(group):
            one_block(i * group + u)
        return carry

    lax.fori_loop(0, nq // group, body, 0)


def _attn_c(proj, slopes, g_out, *, batch, seq, heads, col0, tq=256, group=8):
    for _, dil in DILATED_BRANCHES:
        assert dil & (dil - 1) == 0
    span = max((w // (2 * d)) * d for w, d in DILATED_BRANCHES)
    nwin = 2 * (span // tq) + 1
    nq = seq // tq
    assert span % tq == 0 and nq >= nwin and nq % group == 0
    qb = col0 // HEAD_DIM
    kb, vb = qb + heads, qb + 2 * heads
    ncx = 2 * (nwin - 1) + 1
    smem = pl.BlockSpec(memory_space=pltpu.SMEM)
    return pl.pallas_call(
        functools.partial(_attn_c_kernel, tq=tq, seq=seq, nwin=nwin, group=group),
        grid=(heads, batch),
        in_specs=[smem,
                  pl.BlockSpec((seq, HEAD_DIM), lambda h, b: (b, qb + h)),
                  pl.BlockSpec((seq, HEAD_DIM), lambda h, b: (b, kb + h)),
                  pl.BlockSpec((seq, HEAD_DIM), lambda h, b: (b, vb + h)),
                  pl.BlockSpec((1, HEAD_DIM), lambda h, b: (0, 0))],
        out_specs=pl.BlockSpec((seq, HEAD_DIM), lambda h, b: (b, h)),
        out_shape=jax.ShapeDtypeStruct((batch * seq, heads * HEAD_DIM), BF16),
        scratch_shapes=[pltpu.VMEM((ncx, tq, tq), F32), pltpu.VMEM((ncx, tq, tq), F32),
                        pltpu.VMEM((ncx, tq, tq), F32), pltpu.VMEM((seq, 2 * HEAD_DIM), BF16)],
        compiler_params=_params(("arbitrary", "arbitrary")),
        name="attn_c",
    )(slopes, proj, proj, proj, g_out.reshape(1, HEAD_DIM))


def _outproj_kernel(a_ref, b_ref, c_ref, w_ref, x_ref, o_ref, *, splits):
    acc = x_ref[...]
    row = 0
    for part, width in zip((a_ref, b_ref, c_ref), splits):
        acc = acc + jnp.dot(part[...], w_ref[0, row:row + width, :].astype(BF16),
                            preferred_element_type=F32)
        row += width
    o_ref[...] = acc


def _outproj(ma, mb, mc, w_all, layer, x2, tm=1024, tn=512):
    m, n = x2.shape
    splits = (ma.shape[1], mb.shape[1], mc.shape[1])
    k = sum(splits)
    return pl.pallas_call(
        functools.partial(_outproj_kernel, splits=splits),
        grid=(m // tm, n // tn),
        in_specs=[pl.BlockSpec((tm, splits[0]), lambda i, j: (i, 0)),
                  pl.BlockSpec((tm, splits[1]), lambda i, j: (i, 0)),
                  pl.BlockSpec((tm, splits[2]), lambda i, j: (i, 0)),
                  pl.BlockSpec((1, k, tn), lambda i, j: (layer, 0, j)),
                  pl.BlockSpec((tm, tn), lambda i, j: (i, j))],
        out_specs=pl.BlockSpec((tm, tn), lambda i, j: (i, j)),
        out_shape=jax.ShapeDtypeStruct((m, n), F32),
        compiler_params=_params(("arbitrary", "arbitrary")),
        name="outproj",
    )(ma, mb, mc, w_all, x2)


def _ffn_kernel(x_hbm, g_ref, wg_ref, wu_ref, wd_ref, o_hbm, acc_ref, h_ref, sem, *, tm, norm_rows):
    i = pl.program_id(0)
    j = pl.program_id(1)
    rows_i = pl.ds(pl.multiple_of(i * tm, tm), tm)

    n_chunks = tm // norm_rows

    def load_x(r):
        return pltpu.make_async_copy(
            x_hbm.at[pl.ds(pl.multiple_of(i * tm + r * norm_rows, norm_rows), norm_rows), :],
            acc_ref.at[pl.ds(pl.multiple_of(r * norm_rows, norm_rows), norm_rows), :],
            sem.at[r % 2])

    @pl.when(j == 0)
    def _():
        load_x(0).start()

        def norm(r, carry):
            load_x(r).wait()

            @pl.when(r + 1 < n_chunks)
            def _():
                load_x(r + 1).start()

            rows = pl.ds(pl.multiple_of(r * norm_rows, norm_rows), norm_rows)
            h_ref[rows, :] = (_rms(acc_ref[rows, :]) * g_ref[...]).astype(h_ref.dtype)
            return carry

        lax.fori_loop(0, n_chunks, norm, 0)

    h = h_ref[...]
    gate = jnp.dot(h, wg_ref[0].astype(BF16), preferred_element_type=F32)
    up = jnp.dot(h, wu_ref[0].astype(BF16), preferred_element_type=F32)
    act = (gate * jax.nn.sigmoid(gate) * up).astype(BF16)
    acc_ref[...] += jnp.dot(act, wd_ref[0].astype(BF16), preferred_element_type=F32)

    @pl.when(j == pl.num_programs(1) - 1)
    def _():
        store_o = pltpu.make_async_copy(acc_ref, o_hbm.at[rows_i, :], sem.at[2])
        store_o.start()
        store_o.wait()


def _ffn(x2, g, wg_all, wu_all, wd_all, layer, tm=1024, tf=256, norm_rows=128):
    m, d = x2.shape
    f = wg_all.shape[2]
    return pl.pallas_call(
        functools.partial(_ffn_kernel, tm=tm, norm_rows=norm_rows),
        grid=(m // tm, f // tf),
        in_specs=[pl.BlockSpec(memory_space=pl.ANY),
                  pl.BlockSpec((1, d), lambda i, j: (0, 0)),
                  pl.BlockSpec((1, d, tf), lambda i, j: (layer, 0, j)),
                  pl.BlockSpec((1, d, tf), lambda i, j: (layer, 0, j)),
                  pl.BlockSpec((1, tf, d), lambda i, j: (layer, j, 0))],
        out_specs=pl.BlockSpec(memory_space=pl.ANY),
        out_shape=jax.ShapeDtypeStruct((m, d), F32),
        scratch_shapes=[pltpu.VMEM((tm, d), F32), pltpu.VMEM((tm, d), BF16),
                        pltpu.SemaphoreType.DMA((3,))],
        compiler_params=pltpu.CompilerParams(dimension_semantics=("arbitrary", "arbitrary"),
                                             vmem_limit_bytes=FFN_VMEM_LIMIT),
        name="ffn",
    )(x2, g.reshape(1, d), wg_all, wu_all, wd_all)


def _alibi_slopes(n):
    return jnp.exp2(-8.0 * jnp.arange(1, n + 1, dtype=F32) / n)


def kernel(x, norm1_g, w_in, a_q_g, a_k_g, lambda_q1, lambda_k1, lambda_q2, lambda_k2, a_out_g,
           b_q_g, b_k_g, b_rpb, b_out_g, c_q_g, c_k_g, c_out_g, w_out, norm2_g, w_gate, w_up, w_down):
    batch, seq, d_model = x.shape
    depth = w_in.shape[0]
    n_slots = d_model // HEAD_DIM
    a_heads = n_slots // 8
    b_heads = (n_slots - 2 * a_heads) // 2
    c_heads = n_slots - 2 * a_heads - b_heads
    a_w, b_w, c_w = a_heads * 2 * HEAD_DIM, b_heads * HEAD_DIM, c_heads * HEAD_DIM
    col_a, col_b, col_c = 0, 3 * a_w, 3 * a_w + 3 * b_w
    tn = 512
    norm_tiles = tuple((c // tn, (c + 2 * w) // tn) for c, w in ((col_a, a_w), (col_b, b_w), (col_c, c_w)))
    scale = HEAD_DIM ** -0.5
    slopes_a = _alibi_slopes(a_heads)
    slopes_c = _alibi_slopes(c_heads)

    x2 = x.reshape(batch * seq, d_model)
    for l in range(depth):
        lam_init = 0.8 - 0.6 * math.exp(-0.3 * l)
        gcols = jnp.concatenate([
            jnp.tile(a_q_g[l] * scale, 2 * a_heads), jnp.tile(a_k_g[l], 2 * a_heads), jnp.ones((a_w,), F32),
            jnp.tile(b_q_g[l] * scale, b_heads), jnp.tile(b_k_g[l], b_heads), jnp.ones((b_w,), F32),
            jnp.tile(c_q_g[l] * scale, c_heads), jnp.tile(c_k_g[l], c_heads), jnp.ones((c_w,), F32),
        ]).astype(F32).reshape(1, -1)
        lamv = jnp.stack([lambda_q1[l], lambda_k1[l], lambda_q2[l], lambda_k2[l]]).astype(F32)

        h = _rmsnorm(x2, norm1_g[l])
        proj = _inproj(h, w_in, l, gcols, norm_tiles, tn=tn)
        ma = _attn_a(proj, slopes_a, lamv, a_out_g[l], batch=batch, seq=seq, heads=a_heads,
                     col0=col_a, lam_init=lam_init)
        mb = _attn_b(proj, _nbr_bias_table(b_rpb[l]), b_out_g[l], batch=batch, seq=seq,
                     heads=b_heads, col0=col_b)
        mc = _attn_c(proj, slopes_c, c_out_g[l], batch=batch, seq=seq, heads=c_heads, col0=col_c)
        x2 = _outproj(ma, mb, mc, w_out, l, x2)
        x2 = _ffn(x2, norm2_g[l], w_gate, w_up, w_down, l)
    return x2.reshape(batch, seq, d_model)
```

```python
import functools
import math

import jax
import jax.numpy as jnp
from jax import lax
from jax.experimental import pallas as pl
from jax.experimental.pallas import tpu as pltpu

F32 = jnp.float32
BF16 = jnp.bfloat16

HEAD_DIM = 128
GRID_W = 64
NA_ROWS = 8
NA_COLS = 16
DILATED_BRANCHES = ((128, 1), (512, 4), (2048, 16))
RMS_EPS = 1e-6
NEG = -1e30
LOG2E = math.log2(math.e)

VMEM_LIMIT = 56 * 1024 * 1024
FFN_VMEM_LIMIT = 60 * 1024 * 1024

_NT = (((1,), (1,)), ((), ()))


def _rms(y):
    return y * lax.rsqrt(jnp.mean(y * y, axis=-1, keepdims=True) + RMS_EPS)


def _params(sem):
    return pltpu.CompilerParams(dimension_semantics=sem, vmem_limit_bytes=VMEM_LIMIT)


def _inproj_kernel(x_hbm, g1_ref, w_ref, g_ref, o_ref, h_ref, xs_ref, sem,
                   *, norm_tiles, tm, tn, norm_rows):
    i = pl.program_id(0)
    j = pl.program_id(1)
    n_chunks = tm // norm_rows

    def load_x(r):
        return pltpu.make_async_copy(
            x_hbm.at[pl.ds(pl.multiple_of(i * tm + r * norm_rows, norm_rows), norm_rows), :],
            xs_ref.at[r % 2], sem.at[r % 2])

    @pl.when(j == 0)
    def _():
        load_x(0).start()

        def norm(r, carry):
            load_x(r).wait()

            @pl.when(r + 1 < n_chunks)
            def _():
                load_x(r + 1).start()

            rows = pl.ds(pl.multiple_of(r * norm_rows, norm_rows), norm_rows)
            h_ref[rows, :] = (_rms(xs_ref[r % 2]) * g1_ref[...]).astype(h_ref.dtype)
            return carry

        lax.fori_loop(0, n_chunks, norm, 0)

    w = w_ref[0].astype(BF16)
    is_norm = functools.reduce(jnp.logical_or, [(j >= lo) & (j < hi) for lo, hi in norm_tiles])
    quarter = tm // 4
    for r in range(4):
        rows = slice(r * quarter, (r + 1) * quarter)
        acc = jnp.dot(h_ref[rows, :], w, preferred_element_type=F32)
        for c in range(tn // HEAD_DIM):
            sl = slice(c * HEAD_DIM, (c + 1) * HEAD_DIM)
            y = acc[:, sl]
            inv = lax.rsqrt(jnp.mean(y * y, axis=-1, keepdims=True) + RMS_EPS)
            o_ref[rows, sl] = (y * (jnp.where(is_norm, inv, 1.0) * g_ref[:, sl])).astype(o_ref.dtype)


def _inproj(x2, g1, w_all, layer, gcols, norm_tiles, tm=1024, tn=512, norm_rows=128):
    m, k = x2.shape
    n = w_all.shape[2]
    return pl.pallas_call(
        functools.partial(_inproj_kernel, norm_tiles=norm_tiles, tm=tm, tn=tn, norm_rows=norm_rows),
        grid=(m // tm, n // tn),
        in_specs=[pl.BlockSpec(memory_space=pl.ANY),
                  pl.BlockSpec((1, k), lambda i, j: (0, 0)),
                  pl.BlockSpec((1, k, tn), lambda i, j: (layer, 0, j)),
                  pl.BlockSpec((1, tn), lambda i, j: (0, j))],
        out_specs=pl.BlockSpec((tm, tn), lambda i, j: (i, j)),
        out_shape=jax.ShapeDtypeStruct((m, n), BF16),
        scratch_shapes=[pltpu.VMEM((tm, k), BF16), pltpu.VMEM((2, norm_rows, k), F32),
                        pltpu.SemaphoreType.DMA((2,))],
        compiler_params=_params(("arbitrary", "arbitrary")),
        name="inproj",
    )(x2, g1.reshape(1, k), w_all, gcols)


def _attn_a_kernel(slopes_ref, lamv_ref, q_ref, k_ref, v_ref, g_ref, o_ref, tb_ref,
                   *, tq, tk, seq, lam_init, group):
    h = pl.program_id(0)
    b = pl.program_id(1)
    nq = seq // tq

    @pl.when(b == 0)
    def _():
        slope = slopes_ref[h]
        d0 = (lax.broadcasted_iota(jnp.int32, (tq, tq), 1)
              - lax.broadcasted_iota(jnp.int32, (tq, tq), 0))
        for cx in range(2 * nq - 1):
            delta = d0 + (cx - (nq - 1)) * tq
            tb_ref[cx] = jnp.abs(delta).astype(F32) * (-slope)

    lv = lamv_ref[...]
    lam = (jnp.exp(jnp.sum(lv[0:1] * lv[1:2], axis=-1, keepdims=True))
           - jnp.exp(jnp.sum(lv[2:3] * lv[3:4], axis=-1, keepdims=True)) + lam_init)
    sub = tk // tq

    def one_block(qi):
        q0 = pl.multiple_of(qi * tq, tq)
        q = q_ref[pl.ds(q0, tq), :]
        m_run = l_run = acc = None
        for c in range(seq // tk):
            bias = jnp.concatenate([tb_ref[(c * sub + t) - qi + (nq - 1)] for t in range(sub)],
                                   axis=1)
            s = jnp.concatenate(
                [lax.dot_general(q[:, m * HEAD_DIM:(m + 1) * HEAD_DIM],
                                 k_ref[c * tk:(c + 1) * tk, m * HEAD_DIM:(m + 1) * HEAD_DIM], _NT,
                                 preferred_element_type=F32) + bias for m in range(2)], axis=0)
            mx = jnp.max(s, axis=-1, keepdims=True)
            m_new = mx if c == 0 else jnp.maximum(m_run, mx)
            p = jnp.exp2(s - m_new)
            l_new = jnp.sum(p, axis=-1, keepdims=True)
            pv = jnp.dot(p.astype(BF16), v_ref[c * tk:(c + 1) * tk, :], preferred_element_type=F32)
            if c == 0:
                l_run, acc = l_new, pv
            else:
                alpha = jnp.exp2(m_run - m_new)
                l_run = alpha * l_run + l_new
                acc = alpha * acc + pv
            m_run = m_new
        o = acc[:tq] * (1.0 / l_run[:tq]) - acc[tq:] * (lam / l_run[tq:])
        o_ref[pl.ds(q0, tq), :] = (_rms(o) * (g_ref[...] * (1.0 - lam_init))).astype(o_ref.dtype)

    def body(i, carry):
        for u in range(group):
            one_block(i * group + u)
        return carry

    lax.fori_loop(0, nq // group, body, 0)


def _attn_a(proj, slopes, lamv, g_out, *, batch, seq, heads, col0, lam_init, tq=256, tk=1024,
            group=2):
    vd = 2 * HEAD_DIM
    nq = seq // tq
    assert nq % group == 0 and tk % tq == 0
    qb, kb, vb = (col0 // vd, col0 // vd + heads, col0 // vd + 2 * heads)
    smem = pl.BlockSpec(memory_space=pltpu.SMEM)
    return pl.pallas_call(
        functools.partial(_attn_a_kernel, tq=tq, tk=tk, seq=seq, lam_init=lam_init, group=group),
        grid=(heads, batch),
        in_specs=[smem,
                  pl.BlockSpec((4, HEAD_DIM), lambda h, b: (0, 0)),
                  pl.BlockSpec((seq, vd), lambda h, b: (b, qb + h)),
                  pl.BlockSpec((seq, vd), lambda h, b: (b, kb + h)),
                  pl.BlockSpec((seq, vd), lambda h, b: (b, vb + h)),
                  pl.BlockSpec((1, vd), lambda h, b: (0, 0))],
        out_specs=pl.BlockSpec((seq, vd), lambda h, b: (b, h)),
        out_shape=jax.ShapeDtypeStruct((batch * seq, heads * vd), BF16),
        scratch_shapes=[pltpu.VMEM((2 * nq - 1, tq, tq), F32)],
        compiler_params=_params(("arbitrary", "arbitrary")),
        name="attn_a",
    )(slopes, lamv, proj, proj, proj, g_out.reshape(1, vd))


NB_QROWS = 4
NB_WROWS = NB_QROWS + NA_ROWS


def _nbr_plan(rows):
    variants, var_of_block = [], []
    for rb in range(rows // NB_QROWS):
        r0 = rb * NB_QROWS
        ks = min(max(r0 - NA_ROWS // 2, 0), rows - NB_WROWS)
        starts = tuple(min(max(r - NA_ROWS // 2, 0), rows - NA_ROWS) - ks
                       for r in range(r0, r0 + NB_QROWS))
        assert all(0 <= st and st + NA_ROWS <= NB_WROWS for st in starts)
        pat = (r0 - ks, starts)
        if pat not in variants:
            variants.append(pat)
        var_of_block.append(variants.index(pat))
    return variants, var_of_block


def _attn_b_kernel(q_ref, k_ref, v_ref, tab_ref, g_ref, o_ref, bias_ref, v1_ref, *, rows, group):
    b = pl.program_id(1)
    variants, var_of_block = _nbr_plan(rows)
    tq = NB_QROWS * GRID_W
    tkw = NB_WROWS * GRID_W

    @pl.when(b == 0)
    def _():
        for vi, (off, starts) in enumerate(variants):
            for qr in range(NB_QROWS):
                pieces = []
                for kr in range(NB_WROWS):
                    if starts[qr] <= kr < starts[qr] + NA_ROWS:
                        pieces.append(tab_ref[0, kr - (off + qr) + NA_ROWS - 1])
                    else:
                        pieces.append(jnp.full((GRID_W, GRID_W), NEG, F32))
                bias_ref[vi, qr * GRID_W:(qr + 1) * GRID_W, :] = jnp.concatenate(pieces, axis=1)

    v1_ref[:, :HEAD_DIM] = v_ref[...]
    v1_ref[:, HEAD_DIM:] = jnp.ones((rows * GRID_W, HEAD_DIM), v1_ref.dtype)

    default = max(set(var_of_block), key=var_of_block.count)

    def one_block(rb):
        vi = default
        for j, var in enumerate(var_of_block):
            if var != default:
                vi = jnp.where(rb == j, var, vi)
        ks = jnp.clip(rb * NB_QROWS - NA_ROWS // 2, 0, rows - NB_WROWS)
        q0 = pl.multiple_of(rb * tq, tq)
        k0 = pl.multiple_of(ks * GRID_W, GRID_W)
        s = lax.dot_general(q_ref[pl.ds(q0, tq), :], k_ref[pl.ds(k0, tkw), :], _NT,
                            preferred_element_type=F32) + bias_ref[vi]
        p = jnp.exp2(s - jnp.max(s, axis=-1, keepdims=True))
        ol = jnp.dot(p.astype(BF16), v1_ref[pl.ds(k0, tkw), :], preferred_element_type=F32)
        o = ol[:, :HEAD_DIM] * (1.0 / ol[:, HEAD_DIM:HEAD_DIM + 1])
        o_ref[pl.ds(q0, tq), :] = (_rms(o) * g_ref[...]).astype(o_ref.dtype)

    def body(i, carry):
        for u in range(group):
            one_block(i * group + u)
        return carry

    lax.fori_loop(0, rows // NB_QROWS // group, body, 0)


def _attn_b(proj, tab, g_out, *, batch, seq, heads, col0, group=8):
    rows = seq // GRID_W
    assert rows % (NB_QROWS * group) == 0 and rows >= NB_WROWS
    qb = col0 // HEAD_DIM
    kb, vb = qb + heads, qb + 2 * heads
    nvar = len(_nbr_plan(rows)[0])
    return pl.pallas_call(
        functools.partial(_attn_b_kernel, rows=rows, group=group),
        grid=(heads, batch),
        in_specs=[pl.BlockSpec((seq, HEAD_DIM), lambda h, b: (b, qb + h)),
                  pl.BlockSpec((seq, HEAD_DIM), lambda h, b: (b, kb + h)),
                  pl.BlockSpec((seq, HEAD_DIM), lambda h, b: (b, vb + h)),
                  pl.BlockSpec((1, 2 * NA_ROWS - 1, GRID_W, GRID_W), lambda h, b: (h, 0, 0, 0)),
                  pl.BlockSpec((1, HEAD_DIM), lambda h, b: (0, 0))],
        out_specs=pl.BlockSpec((seq, HEAD_DIM), lambda h, b: (b, h)),
        out_shape=jax.ShapeDtypeStruct((batch * seq, heads * HEAD_DIM), BF16),
        scratch_shapes=[pltpu.VMEM((nvar, NB_QROWS * GRID_W, NB_WROWS * GRID_W), F32),
                        pltpu.VMEM((seq, 2 * HEAD_DIM), BF16)],
        compiler_params=_params(("arbitrary", "arbitrary")),
        name="attn_b",
    )(proj, proj, proj, tab, g_out.reshape(1, HEAD_DIM))


def _nbr_bias_table(rpb):
    cols = jnp.arange(GRID_W)
    c_start = jnp.clip(cols - NA_COLS // 2, 0, GRID_W - NA_COLS)
    col_mask = (cols[None, :] >= c_start[:, None]) & (cols[None, :] < c_start[:, None] + NA_COLS)
    dc = cols[None, :] - cols[:, None] + NA_COLS - 1
    t = jnp.full(rpb.shape[:2] + (GRID_W, GRID_W), NEG, F32)
    for kc in range(2 * NA_COLS - 1):
        t = jnp.where((dc == kc) & col_mask, rpb[:, :, kc, None, None].astype(F32) * LOG2E, t)
    return t


def _attn_c_kernel(slopes_ref, q_ref, k_ref, v_ref, g_ref, o_ref, dist_ref, logm_ref, tb_ref, v1_ref,
                   *, tq, seq, nwin, group):
    h = pl.program_id(0)
    b = pl.program_id(1)
    nq = seq // tq
    reach = (nwin - 1) // 2
    ncx = 2 * (nwin - 1) + 1

    @pl.when((h == 0) & (b == 0))
    def _():
        d0 = (lax.broadcasted_iota(jnp.int32, (tq, tq), 1)
              - lax.broadcasted_iota(jnp.int32, (tq, tq), 0))
        for cx in range(ncx):
            delta = d0 + (cx - (nwin - 1)) * tq
            ad = jnp.abs(delta)
            mult = jnp.zeros((tq, tq), F32)
            for window, dil in DILATED_BRANCHES:
                radius = window // (2 * dil)
                hit = ((delta & (dil - 1)) == 0) & (ad <= radius * dil)
                mult = mult + hit.astype(F32)
            dist_ref[cx] = ad.astype(F32)
            logm_ref[cx] = jnp.where(mult > 0, jnp.log2(jnp.maximum(mult, 1.0)), NEG)

    @pl.when(b == 0)
    def _():
        slope = slopes_ref[h]
        for cx in range(ncx):
            tb_ref[cx] = dist_ref[cx] * (-slope) + logm_ref[cx]

    v1_ref[:, :HEAD_DIM] = v_ref[...]
    v1_ref[:, HEAD_DIM:] = jnp.ones((seq, HEAD_DIM), v1_ref.dtype)

    def one_block(qi):
        ks = jnp.clip(qi - reach, 0, nq - nwin)
        k0 = pl.multiple_of(ks * tq, tq)
        q0 = pl.multiple_of(qi * tq, tq)
        bias = jnp.concatenate([tb_ref[ks + t - qi + (nwin - 1)] for t in range(nwin)], axis=1)
        s = lax.dot_general(q_ref[pl.ds(q0, tq), :], k_ref[pl.ds(k0, nwin * tq), :], _NT,
                            preferred_element_type=F32) + bias
        p = jnp.exp2(s - jnp.max(s, axis=-1, keepdims=True))
        ol = jnp.dot(p.astype(BF16), v1_ref[pl.ds(k0, nwin * tq), :], preferred_element_type=F32)
        o = ol[:, :HEAD_DIM] * (1.0 / ol[:, HEAD_DIM:HEAD_DIM + 1])
        o_ref[pl.ds(q0, tq), :] = (_rms(o) * g_ref[...]).astype(o_ref.dtype)

    def body(i, carry):
        for u in range(group):
            one_block(i * group + u)
        return carry

    lax.fori_loop(0, nq // group, body, 0)


def _attn_c(proj, slopes, g_out, *, batch, seq, heads, col0, tq=256, group=8):
    for _, dil in DILATED_BRANCHES:
        assert dil & (dil - 1) == 0
    span = max((w // (2 * d)) * d for w, d in DILATED_BRANCHES)
    nwin = 2 * (span // tq) + 1
    nq = seq // tq
    assert span % tq == 0 and nq >= nwin and nq % group == 0
    qb = col0 // HEAD_DIM
    kb, vb = qb + heads, qb + 2 * heads
    ncx = 2 * (nwin - 1) + 1
    smem = pl.BlockSpec(memory_space=pltpu.SMEM)
    return pl.pallas_call(
        functools.partial(_attn_c_kernel, tq=tq, seq=seq, nwin=nwin, group=group),
        grid=(heads, batch),
        in_specs=[smem,
                  pl.BlockSpec((seq, HEAD_DIM), lambda h, b: (b, qb + h)),
                  pl.BlockSpec((seq, HEAD_DIM), lambda h, b: (b, kb + h)),
                  pl.BlockSpec((seq, HEAD_DIM), lambda h, b: (b, vb + h)),
                  pl.BlockSpec((1, HEAD_DIM), lambda h, b: (0, 0))],
        out_specs=pl.BlockSpec((seq, HEAD_DIM), lambda h, b: (b, h)),
        out_shape=jax.ShapeDtypeStruct((batch * seq, heads * HEAD_DIM), BF16),
        scratch_shapes=[pltpu.VMEM((ncx, tq, tq), F32), pltpu.VMEM((ncx, tq, tq), F32),
                        pltpu.VMEM((ncx, tq, tq), F32), pltpu.VMEM((seq, 2 * HEAD_DIM), BF16)],
        compiler_params=_params(("arbitrary", "arbitrary")),
        name="attn_c",
    )(slopes, proj, proj, proj, g_out.reshape(1, HEAD_DIM))


def _outproj_kernel(a_ref, b_ref, c_ref, w_ref, x_ref, o_ref, *, splits):
    acc = x_ref[...]
    row = 0
    for part, width in zip((a_ref, b_ref, c_ref), splits):
        acc = acc + jnp.dot(part[...], w_ref[0, row:row + width, :].astype(BF16),
                            preferred_element_type=F32)
        row += width
    o_ref[...] = acc


def _outproj(ma, mb, mc, w_all, layer, x2, tm=1024, tn=512):
    m, n = x2.shape
    splits = (ma.shape[1], mb.shape[1], mc.shape[1])
    k = sum(splits)
    return pl.pallas_call(
        functools.partial(_outproj_kernel, splits=splits),
        grid=(m // tm, n // tn),
        in_specs=[pl.BlockSpec((tm, splits[0]), lambda i, j: (i, 0)),
                  pl.BlockSpec((tm, splits[1]), lambda i, j: (i, 0)),
                  pl.BlockSpec((tm, splits[2]), lambda i, j: (i, 0)),
                  pl.BlockSpec((1, k, tn), lambda i, j: (layer, 0, j)),
                  pl.BlockSpec((tm, tn), lambda i, j: (i, j))],
        out_specs=pl.BlockSpec((tm, tn), lambda i, j: (i, j)),
        out_shape=jax.ShapeDtypeStruct((m, n), F32),
        compiler_params=_params(("arbitrary", "arbitrary")),
        name="outproj",
    )(ma, mb, mc, w_all, x2)


def _ffn_kernel(x_hbm, g_ref, wg_ref, wu_ref, wd_ref, o_hbm, acc_ref, h_ref, sem, *, tm, norm_rows):
    i = pl.program_id(0)
    j = pl.program_id(1)
    rows_i = pl.ds(pl.multiple_of(i * tm, tm), tm)

    n_chunks = tm // norm_rows

    def load_x(r):
        return pltpu.make_async_copy(
            x_hbm.at[pl.ds(pl.multiple_of(i * tm + r * norm_rows, norm_rows), norm_rows), :],
            acc_ref.at[pl.ds(pl.multiple_of(r * norm_rows, norm_rows), norm_rows), :],
            sem.at[r % 2])

    @pl.when(j == 0)
    def _():
        load_x(0).start()

        def norm(r, carry):
            load_x(r).wait()

            @pl.when(r + 1 < n_chunks)
            def _():
                load_x(r + 1).start()

            rows = pl.ds(pl.multiple_of(r * norm_rows, norm_rows), norm_rows)
            h_ref[rows, :] = (_rms(acc_ref[rows, :]) * g_ref[...]).astype(h_ref.dtype)
            return carry

        lax.fori_loop(0, n_chunks, norm, 0)

    h = h_ref[...]
    gate = jnp.dot(h, wg_ref[0].astype(BF16), preferred_element_type=F32)
    up = jnp.dot(h, wu_ref[0].astype(BF16), preferred_element_type=F32)
    act = (gate * jax.nn.sigmoid(gate) * up).astype(BF16)
    acc_ref[...] += jnp.dot(act, wd_ref[0].astype(BF16), preferred_element_type=F32)

    @pl.when(j == pl.num_programs(1) - 1)
    def _():
        store_o = pltpu.make_async_copy(acc_ref, o_hbm.at[rows_i, :], sem.at[2])
        store_o.start()
        store_o.wait()


def _ffn(x2, g, wg_all, wu_all, wd_all, layer, tm=1024, tf=256, norm_rows=128):
    m, d = x2.shape
    f = wg_all.shape[2]
    return pl.pallas_call(
        functools.partial(_ffn_kernel, tm=tm, norm_rows=norm_rows),
        grid=(m // tm, f // tf),
        in_specs=[pl.BlockSpec(memory_space=pl.ANY),
                  pl.BlockSpec((1, d), lambda i, j: (0, 0)),
                  pl.BlockSpec((1, d, tf), lambda i, j: (layer, 0, j)),
                  pl.BlockSpec((1, d, tf), lambda i, j: (layer, 0, j)),
                  pl.BlockSpec((1, tf, d), lambda i, j: (layer, j, 0))],
        out_specs=pl.BlockSpec(memory_space=pl.ANY),
        out_shape=jax.ShapeDtypeStruct((m, d), F32),
        scratch_shapes=[pltpu.VMEM((tm, d), F32), pltpu.VMEM((tm, d), BF16),
                        pltpu.SemaphoreType.DMA((3,))],
        compiler_params=pltpu.CompilerParams(dimension_semantics=("arbitrary", "arbitrary"),
                                             vmem_limit_bytes=FFN_VMEM_LIMIT),
        name="ffn",
    )(x2, g.reshape(1, d), wg_all, wu_all, wd_all)


def _alibi_slopes(n):
    return jnp.exp2(-8.0 * jnp.arange(1, n + 1, dtype=F32) / n)


def kernel(x, norm1_g, w_in, a_q_g, a_k_g, lambda_q1, lambda_k1, lambda_q2, lambda_k2, a_out_g,
           b_q_g, b_k_g, b_rpb, b_out_g, c_q_g, c_k_g, c_out_g, w_out, norm2_g, w_gate, w_up, w_down):
    batch, seq, d_model = x.shape
    depth = w_in.shape[0]
    n_slots = d_model // HEAD_DIM
    a_heads = n_slots // 8
    b_heads = (n_slots - 2 * a_heads) // 2
    c_heads = n_slots - 2 * a_heads - b_heads
    a_w, b_w, c_w = a_heads * 2 * HEAD_DIM, b_heads * HEAD_DIM, c_heads * HEAD_DIM
    col_a, col_b, col_c = 0, 3 * a_w, 3 * a_w + 3 * b_w
    tn = 512
    norm_tiles = tuple((c // tn, (c + 2 * w) // tn) for c, w in ((col_a, a_w), (col_b, b_w), (col_c, c_w)))
    scale = HEAD_DIM ** -0.5 * LOG2E
    slopes_a = _alibi_slopes(a_heads) * LOG2E
    slopes_c = _alibi_slopes(c_heads) * LOG2E

    x2 = x.reshape(batch * seq, d_model)
    for l in range(depth):
        lam_init = 0.8 - 0.6 * math.exp(-0.3 * l)
        gcols = jnp.concatenate([
            jnp.tile(a_q_g[l] * scale, 2 * a_heads), jnp.tile(a_k_g[l], 2 * a_heads), jnp.ones((a_w,), F32),
            jnp.tile(b_q_g[l] * scale, b_heads), jnp.tile(b_k_g[l], b_heads), jnp.ones((b_w,), F32),
            jnp.tile(c_q_g[l] * scale, c_heads), jnp.tile(c_k_g[l], c_heads), jnp.ones((c_w,), F32),
        ]).astype(F32).reshape(1, -1)
        lamv = jnp.stack([lambda_q1[l], lambda_k1[l], lambda_q2[l], lambda_k2[l]]).astype(F32)

        proj = _inproj(x2, norm1_g[l], w_in, l, gcols, norm_tiles, tn=tn)
        ma = _attn_a(proj, slopes_a, lamv, a_out_g[l], batch=batch, seq=seq, heads=a_heads,
                     col0=col_a, lam_init=lam_init)
        mb = _attn_b(proj, _nbr_bias_table(b_rpb[l]), b_out_g[l], batch=batch, seq=seq,
                     heads=b_heads, col0=col_b)
        mc = _attn_c(proj, slopes_c, c_out_g[l], batch=batch, seq=seq, heads=c_heads, col0=col_c)
        x2 = _outproj(ma, mb, mc, w_out, l, x2)
        x2 = _ffn(x2, norm2_g[l], w_gate, w_up, w_down, l)
    return x2.reshape(batch, seq, d_model)
```

```python
import functools
import math

import jax
import jax.numpy as jnp
from jax import lax
from jax.experimental import pallas as pl
from jax.experimental.pallas import tpu as pltpu

F32 = jnp.float32
BF16 = jnp.bfloat16

HEAD_DIM = 128
GRID_W = 64
NA_ROWS = 8
NA_COLS = 16
DILATED_BRANCHES = ((128, 1), (512, 4), (2048, 16))
RMS_EPS = 1e-6
NEG = -1e30
LOG2E = math.log2(math.e)

VMEM_LIMIT = 56 * 1024 * 1024
FFN_VMEM_LIMIT = 60 * 1024 * 1024
X_RING_SLOTS = 4
FFN_STORE_CHUNKS = 4

_NT = (((1,), (1,)), ((), ()))


def _rms(y):
    return y * lax.rsqrt(jnp.mean(y * y, axis=-1, keepdims=True) + RMS_EPS)


def _params(sem):
    return pltpu.CompilerParams(dimension_semantics=sem, vmem_limit_bytes=VMEM_LIMIT)


def _inproj_kernel(x_hbm, g1_ref, w_ref, g_ref, o_ref, h_ref, xs_ref, sem,
                   *, norm_tiles, tm, tn, norm_rows):
    i = pl.program_id(0)
    j = pl.program_id(1)
    n_chunks = tm // norm_rows

    n_slots = xs_ref.shape[0]

    def load_x(r):
        return pltpu.make_async_copy(
            x_hbm.at[pl.ds(pl.multiple_of(i * tm + r * norm_rows, norm_rows), norm_rows), :],
            xs_ref.at[r % n_slots], sem.at[r % n_slots])

    @pl.when(j == 0)
    def _():
        for r in range(n_slots - 1):
            load_x(r).start()

        def norm(r, carry):
            load_x(r).wait()

            @pl.when(r + n_slots - 1 < n_chunks)
            def _():
                load_x(r + n_slots - 1).start()

            rows = pl.ds(pl.multiple_of(r * norm_rows, norm_rows), norm_rows)
            h_ref[rows, :] = (_rms(xs_ref[r % n_slots]) * g1_ref[...]).astype(h_ref.dtype)
            return carry

        lax.fori_loop(0, n_chunks, norm, 0)

    w = w_ref[0].astype(BF16)
    is_norm = functools.reduce(jnp.logical_or, [(j >= lo) & (j < hi) for lo, hi in norm_tiles])
    acc = jnp.dot(h_ref[...], w, preferred_element_type=F32)
    for c in range(tn // HEAD_DIM):
        sl = slice(c * HEAD_DIM, (c + 1) * HEAD_DIM)
        y = acc[:, sl]
        inv = lax.rsqrt(jnp.mean(y * y, axis=-1, keepdims=True) + RMS_EPS)
        o_ref[:, sl] = (y * (jnp.where(is_norm, inv, 1.0) * g_ref[:, sl])).astype(o_ref.dtype)


def _inproj(x2, g1, w_all, layer, gcols, norm_tiles, tm=1024, tn=512, norm_rows=128):
    m, k = x2.shape
    n = w_all.shape[2]
    return pl.pallas_call(
        functools.partial(_inproj_kernel, norm_tiles=norm_tiles, tm=tm, tn=tn, norm_rows=norm_rows),
        grid=(m // tm, n // tn),
        in_specs=[pl.BlockSpec(memory_space=pl.ANY),
                  pl.BlockSpec((1, k), lambda i, j: (0, 0)),
                  pl.BlockSpec((1, k, tn), lambda i, j: (layer, 0, j)),
                  pl.BlockSpec((1, tn), lambda i, j: (0, j))],
        out_specs=pl.BlockSpec((tm, tn), lambda i, j: (i, j)),
        out_shape=jax.ShapeDtypeStruct((m, n), BF16),
        scratch_shapes=[pltpu.VMEM((tm, k), BF16), pltpu.VMEM((X_RING_SLOTS, norm_rows, k), F32),
                        pltpu.SemaphoreType.DMA((X_RING_SLOTS,))],
        compiler_params=_params(("arbitrary", "arbitrary")),
        name="inproj",
    )(x2, g1.reshape(1, k), w_all, gcols)


def _attn_a_kernel(slopes_ref, lamv_ref, q_ref, k_ref, v_ref, g_ref, o_ref, tb_ref,
                   *, tq, tk, seq, lam_init, group):
    h = pl.program_id(0)
    b = pl.program_id(1)
    nq = seq // tq

    @pl.when(b == 0)
    def _():
        slope = slopes_ref[h]
        d0 = (lax.broadcasted_iota(jnp.int32, (tq, tq), 1)
              - lax.broadcasted_iota(jnp.int32, (tq, tq), 0))
        for cx in range(2 * nq - 1):
            delta = d0 + (cx - (nq - 1)) * tq
            tb_ref[cx] = jnp.abs(delta).astype(F32) * (-slope)

    lv = lamv_ref[...]
    lam = (jnp.exp(jnp.sum(lv[0:1] * lv[1:2], axis=-1, keepdims=True))
           - jnp.exp(jnp.sum(lv[2:3] * lv[3:4], axis=-1, keepdims=True)) + lam_init)
    sub = tk // tq

    def one_block(qi):
        q0 = pl.multiple_of(qi * tq, tq)
        q = q_ref[pl.ds(q0, tq), :]
        m_run = l_run = acc = None
        for c in range(seq // tk):
            bias = jnp.concatenate([tb_ref[(c * sub + t) - qi + (nq - 1)] for t in range(sub)],
                                   axis=1)
            s = jnp.concatenate(
                [lax.dot_general(q[:, m * HEAD_DIM:(m + 1) * HEAD_DIM],
                                 k_ref[c * tk:(c + 1) * tk, m * HEAD_DIM:(m + 1) * HEAD_DIM], _NT,
                                 preferred_element_type=F32) + bias for m in range(2)], axis=0)
            mx = jnp.max(s, axis=-1, keepdims=True)
            m_new = mx if c == 0 else jnp.maximum(m_run, mx)
            p = jnp.exp2(s - m_new)
            l_new = jnp.sum(p, axis=-1, keepdims=True)
            pv = jnp.dot(p.astype(BF16), v_ref[c * tk:(c + 1) * tk, :], preferred_element_type=F32)
            if c == 0:
                l_run, acc = l_new, pv
            else:
                alpha = jnp.exp2(m_run - m_new)
                l_run = alpha * l_run + l_new
                acc = alpha * acc + pv
            m_run = m_new
        o = acc[:tq] * (1.0 / l_run[:tq]) - acc[tq:] * (lam / l_run[tq:])
        o_ref[pl.ds(q0, tq), :] = (_rms(o) * (g_ref[...] * (1.0 - lam_init))).astype(o_ref.dtype)

    def body(i, carry):
        for u in range(group):
            one_block(i * group + u)
        return carry

    lax.fori_loop(0, nq // group, body, 0)


def _attn_a(proj, slopes, lamv, g_out, *, batch, seq, heads, col0, lam_init, tq=256, tk=1024,
            group=2):
    vd = 2 * HEAD_DIM
    nq = seq // tq
    assert nq % group == 0 and tk % tq == 0
    qb, kb, vb = (col0 // vd, col0 // vd + heads, col0 // vd + 2 * heads)
    smem = pl.BlockSpec(memory_space=pltpu.SMEM)
    return pl.pallas_call(
        functools.partial(_attn_a_kernel, tq=tq, tk=tk, seq=seq, lam_init=lam_init, group=group),
        grid=(heads, batch),
        in_specs=[smem,
                  pl.BlockSpec((4, HEAD_DIM), lambda h, b: (0, 0)),
                  pl.BlockSpec((seq, vd), lambda h, b: (b, qb + h)),
                  pl.BlockSpec((seq, vd), lambda h, b: (b, kb + h)),
                  pl.BlockSpec((seq, vd), lambda h, b: (b, vb + h)),
                  pl.BlockSpec((1, vd), lambda h, b: (0, 0))],
        out_specs=pl.BlockSpec((seq, vd), lambda h, b: (b, h)),
        out_shape=jax.ShapeDtypeStruct((batch * seq, heads * vd), BF16),
        scratch_shapes=[pltpu.VMEM((2 * nq - 1, tq, tq), F32)],
        compiler_params=_params(("arbitrary", "arbitrary")),
        name="attn_a",
    )(slopes, lamv, proj, proj, proj, g_out.reshape(1, vd))


NB_QROWS = 4
NB_WROWS = NB_QROWS + NA_ROWS


def _nbr_plan(rows):
    variants, var_of_block = [], []
    for rb in range(rows // NB_QROWS):
        r0 = rb * NB_QROWS
        ks = min(max(r0 - NA_ROWS // 2, 0), rows - NB_WROWS)
        starts = tuple(min(max(r - NA_ROWS // 2, 0), rows - NA_ROWS) - ks
                       for r in range(r0, r0 + NB_QROWS))
        assert all(0 <= st and st + NA_ROWS <= NB_WROWS for st in starts)
        pat = (r0 - ks, starts)
        if pat not in variants:
            variants.append(pat)
        var_of_block.append(variants.index(pat))
    return variants, var_of_block


def _attn_b_kernel(q_ref, k_ref, v_ref, tab_ref, g_ref, o_ref, bias_ref, v1_ref, *, rows, group):
    b = pl.program_id(1)
    variants, var_of_block = _nbr_plan(rows)
    tq = NB_QROWS * GRID_W
    tkw = NB_WROWS * GRID_W

    @pl.when(b == 0)
    def _():
        for vi, (off, starts) in enumerate(variants):
            for qr in range(NB_QROWS):
                pieces = []
                for kr in range(NB_WROWS):
                    if starts[qr] <= kr < starts[qr] + NA_ROWS:
                        pieces.append(tab_ref[0, kr - (off + qr) + NA_ROWS - 1])
                    else:
                        pieces.append(jnp.full((GRID_W, GRID_W), NEG, F32))
                bias_ref[vi, qr * GRID_W:(qr + 1) * GRID_W, :] = jnp.concatenate(pieces, axis=1)

    v1_ref[:, :HEAD_DIM] = v_ref[...]
    v1_ref[:, HEAD_DIM:] = jnp.ones((rows * GRID_W, HEAD_DIM), v1_ref.dtype)

    default = max(set(var_of_block), key=var_of_block.count)

    def one_block(rb):
        vi = default
        for j, var in enumerate(var_of_block):
            if var != default:
                vi = jnp.where(rb == j, var, vi)
        ks = jnp.clip(rb * NB_QROWS - NA_ROWS // 2, 0, rows - NB_WROWS)
        q0 = pl.multiple_of(rb * tq, tq)
        k0 = pl.multiple_of(ks * GRID_W, GRID_W)
        s = lax.dot_general(q_ref[pl.ds(q0, tq), :], k_ref[pl.ds(k0, tkw), :], _NT,
                            preferred_element_type=F32) + bias_ref[vi]
        p = jnp.exp2(s - jnp.max(s, axis=-1, keepdims=True))
        ol = jnp.dot(p.astype(BF16), v1_ref[pl.ds(k0, tkw), :], preferred_element_type=F32)
        o = ol[:, :HEAD_DIM] * (1.0 / ol[:, HEAD_DIM:HEAD_DIM + 1])
        o_ref[pl.ds(q0, tq), :] = (_rms(o) * g_ref[...]).astype(o_ref.dtype)

    def body(i, carry):
        for u in range(group):
            one_block(i * group + u)
        return carry

    lax.fori_loop(0, rows // NB_QROWS // group, body, 0)


def _attn_b(proj, tab, g_out, *, batch, seq, heads, col0, group=8):
    rows = seq // GRID_W
    assert rows % (NB_QROWS * group) == 0 and rows >= NB_WROWS
    qb = col0 // HEAD_DIM
    kb, vb = qb + heads, qb + 2 * heads
    nvar = len(_nbr_plan(rows)[0])
    return pl.pallas_call(
        functools.partial(_attn_b_kernel, rows=rows, group=group),
        grid=(heads, batch),
        in_specs=[pl.BlockSpec((seq, HEAD_DIM), lambda h, b: (b, qb + h)),
                  pl.BlockSpec((seq, HEAD_DIM), lambda h, b: (b, kb + h)),
                  pl.BlockSpec((seq, HEAD_DIM), lambda h, b: (b, vb + h)),
                  pl.BlockSpec((1, 2 * NA_ROWS - 1, GRID_W, GRID_W), lambda h, b: (h, 0, 0, 0)),
                  pl.BlockSpec((1, HEAD_DIM), lambda h, b: (0, 0))],
        out_specs=pl.BlockSpec((seq, HEAD_DIM), lambda h, b: (b, h)),
        out_shape=jax.ShapeDtypeStruct((batch * seq, heads * HEAD_DIM), BF16),
        scratch_shapes=[pltpu.VMEM((nvar, NB_QROWS * GRID_W, NB_WROWS * GRID_W), F32),
                        pltpu.VMEM((seq, 2 * HEAD_DIM), BF16)],
        compiler_params=_params(("arbitrary", "arbitrary")),
        name="attn_b",
    )(proj, proj, proj, tab, g_out.reshape(1, HEAD_DIM))


def _nbr_bias_table(rpb):
    cols = jnp.arange(GRID_W)
    c_start = jnp.clip(cols - NA_COLS // 2, 0, GRID_W - NA_COLS)
    col_mask = (cols[None, :] >= c_start[:, None]) & (cols[None, :] < c_start[:, None] + NA_COLS)
    dc = cols[None, :] - cols[:, None] + NA_COLS - 1
    t = jnp.full(rpb.shape[:2] + (GRID_W, GRID_W), NEG, F32)
    for kc in range(2 * NA_COLS - 1):
        t = jnp.where((dc == kc) & col_mask, rpb[:, :, kc, None, None].astype(F32) * LOG2E, t)
    return t


def _attn_c_kernel(slopes_ref, q_ref, k_ref, v_ref, g_ref, o_ref, dist_ref, logm_ref, tb_ref, v1_ref,
                   *, tq, seq, nwin, group):
    h = pl.program_id(0)
    b = pl.program_id(1)
    nq = seq // tq
    reach = (nwin - 1) // 2
    ncx = 2 * (nwin - 1) + 1

    @pl.when((h == 0) & (b == 0))
    def _():
        d0 = (lax.broadcasted_iota(jnp.int32, (tq, tq), 1)
              - lax.broadcasted_iota(jnp.int32, (tq, tq), 0))
        for cx in range(ncx):
            delta = d0 + (cx - (nwin - 1)) * tq
            ad = jnp.abs(delta)
            mult = jnp.zeros((tq, tq), F32)
            for window, dil in DILATED_BRANCHES:
                radius = window // (2 * dil)
                hit = ((delta & (dil - 1)) == 0) & (ad <= radius * dil)
                mult = mult + hit.astype(F32)
            dist_ref[cx] = ad.astype(F32)
            logm_ref[cx] = jnp.where(mult > 0, jnp.log2(jnp.maximum(mult, 1.0)), NEG)

    @pl.when(b == 0)
    def _():
        slope = slopes_ref[h]
        for cx in range(ncx):
            tb_ref[cx] = dist_ref[cx] * (-slope) + logm_ref[cx]

    v1_ref[:, :HEAD_DIM] = v_ref[...]
    v1_ref[:, HEAD_DIM:] = jnp.ones((seq, HEAD_DIM), v1_ref.dtype)

    def one_block(qi):
        ks = jnp.clip(qi - reach, 0, nq - nwin)
        k0 = pl.multiple_of(ks * tq, tq)
        q0 = pl.multiple_of(qi * tq, tq)
        bias = jnp.concatenate([tb_ref[ks + t - qi + (nwin - 1)] for t in range(nwin)], axis=1)
        s = lax.dot_general(q_ref[pl.ds(q0, tq), :], k_ref[pl.ds(k0, nwin * tq), :], _NT,
                            preferred_element_type=F32) + bias
        p = jnp.exp2(s - jnp.max(s, axis=-1, keepdims=True))
        ol = jnp.dot(p.astype(BF16), v1_ref[pl.ds(k0, nwin * tq), :], preferred_element_type=F32)
        o = ol[:, :HEAD_DIM] * (1.0 / ol[:, HEAD_DIM:HEAD_DIM + 1])
        o_ref[pl.ds(q0, tq), :] = (_rms(o) * g_ref[...]).astype(o_ref.dtype)

    def body(i, carry):
        for u in range(group):
            one_block(i * group + u)
        return carry

    lax.fori_loop(0, nq // group, body, 0)


def _attn_c(proj, slopes, g_out, *, batch, seq, heads, col0, tq=256, group=8):
    for _, dil in DILATED_BRANCHES:
        assert dil & (dil - 1) == 0
    span = max((w // (2 * d)) * d for w, d in DILATED_BRANCHES)
    nwin = 2 * (span // tq) + 1
    nq = seq // tq
    assert span % tq == 0 and nq >= nwin and nq % group == 0
    qb = col0 // HEAD_DIM
    kb, vb = qb + heads, qb + 2 * heads
    ncx = 2 * (nwin - 1) + 1
    smem = pl.BlockSpec(memory_space=pltpu.SMEM)
    return pl.pallas_call(
        functools.partial(_attn_c_kernel, tq=tq, seq=seq, nwin=nwin, group=group),
        grid=(heads, batch),
        in_specs=[smem,
                  pl.BlockSpec((seq, HEAD_DIM), lambda h, b: (b, qb + h)),
                  pl.BlockSpec((seq, HEAD_DIM), lambda h, b: (b, kb + h)),
                  pl.BlockSpec((seq, HEAD_DIM), lambda h, b: (b, vb + h)),
                  pl.BlockSpec((1, HEAD_DIM), lambda h, b: (0, 0))],
        out_specs=pl.BlockSpec((seq, HEAD_DIM), lambda h, b: (b, h)),
        out_shape=jax.ShapeDtypeStruct((batch * seq, heads * HEAD_DIM), BF16),
        scratch_shapes=[pltpu.VMEM((ncx, tq, tq), F32), pltpu.VMEM((ncx, tq, tq), F32),
                        pltpu.VMEM((ncx, tq, tq), F32), pltpu.VMEM((seq, 2 * HEAD_DIM), BF16)],
        compiler_params=_params(("arbitrary", "arbitrary")),
        name="attn_c",
    )(slopes, proj, proj, proj, g_out.reshape(1, HEAD_DIM))


def _outproj_kernel(a_ref, b_ref, c_ref, w_ref, x_ref, o_ref, *, splits):
    acc = x_ref[...]
    row = 0
    for part, width in zip((a_ref, b_ref, c_ref), splits):
        acc = acc + jnp.dot(part[...], w_ref[0, row:row + width, :].astype(BF16),
                            preferred_element_type=F32)
        row += width
    o_ref[...] = acc


def _outproj(ma, mb, mc, w_all, layer, x2, tm=1024, tn=512):
    m, n = x2.shape
    splits = (ma.shape[1], mb.shape[1], mc.shape[1])
    k = sum(splits)
    return pl.pallas_call(
        functools.partial(_outproj_kernel, splits=splits),
        grid=(m // tm, n // tn),
        in_specs=[pl.BlockSpec((tm, splits[0]), lambda i, j: (i, 0)),
                  pl.BlockSpec((tm, splits[1]), lambda i, j: (i, 0)),
                  pl.BlockSpec((tm, splits[2]), lambda i, j: (i, 0)),
                  pl.BlockSpec((1, k, tn), lambda i, j: (layer, 0, j)),
                  pl.BlockSpec((tm, tn), lambda i, j: (i, j))],
        out_specs=pl.BlockSpec((tm, tn), lambda i, j: (i, j)),
        out_shape=jax.ShapeDtypeStruct((m, n), F32),
        compiler_params=_params(("arbitrary", "arbitrary")),
        name="outproj",
    )(ma, mb, mc, w_all, x2)


def _ffn_kernel(x_hbm, g_ref, wg_ref, wu_ref, wd_ref, o_hbm, acc_ref, h_ref, load_sem, store_sem,
                *, tm, norm_rows):
    i = pl.program_id(0)
    j = pl.program_id(1)
    n_chunks = tm // norm_rows
    store_rows = tm // FFN_STORE_CHUNKS

    def load_x(r):
        return pltpu.make_async_copy(
            x_hbm.at[pl.ds(pl.multiple_of(i * tm + r * norm_rows, norm_rows), norm_rows), :],
            acc_ref.at[pl.ds(pl.multiple_of(r * norm_rows, norm_rows), norm_rows), :],
            load_sem.at[r])

    def store_o(r):
        return pltpu.make_async_copy(
            acc_ref.at[pl.ds(r * store_rows, store_rows), :],
            o_hbm.at[pl.ds(pl.multiple_of(i * tm + r * store_rows, store_rows), store_rows), :],
            store_sem.at[r])

    @pl.when(j == 0)
    def _():
        for r in range(n_chunks):
            load_x(r).start()

        def norm(r, carry):
            load_x(r).wait()
            rows = pl.ds(pl.multiple_of(r * norm_rows, norm_rows), norm_rows)
            h_ref[rows, :] = (_rms(acc_ref[rows, :]) * g_ref[...]).astype(h_ref.dtype)
            return carry

        lax.fori_loop(0, n_chunks, norm, 0)

    def step(write_back):
        h = h_ref[...]
        gate = jnp.dot(h, wg_ref[0].astype(BF16), preferred_element_type=F32)
        up = jnp.dot(h, wu_ref[0].astype(BF16), preferred_element_type=F32)
        act = (gate * jax.nn.sigmoid(gate) * up).astype(BF16)
        wd = wd_ref[0].astype(BF16)
        if not write_back:
            acc_ref[...] += jnp.dot(act, wd, preferred_element_type=F32)
            return
        for r in range(FFN_STORE_CHUNKS):
            rows = slice(r * store_rows, (r + 1) * store_rows)
            acc_ref[rows, :] += jnp.dot(act[rows, :], wd, preferred_element_type=F32)
            store_o(r).start()
        for r in range(FFN_STORE_CHUNKS):
            store_o(r).wait()

    is_last = j == pl.num_programs(1) - 1
    pl.when(jnp.logical_not(is_last))(functools.partial(step, False))
    pl.when(is_last)(functools.partial(step, True))


def _ffn(x2, g, wg_all, wu_all, wd_all, layer, tm=1024, tf=256, norm_rows=128):
    m, d = x2.shape
    f = wg_all.shape[2]
    return pl.pallas_call(
        functools.partial(_ffn_kernel, tm=tm, norm_rows=norm_rows),
        grid=(m // tm, f // tf),
        in_specs=[pl.BlockSpec(memory_space=pl.ANY),
                  pl.BlockSpec((1, d), lambda i, j: (0, 0)),
                  pl.BlockSpec((1, d, tf), lambda i, j: (layer, 0, j)),
                  pl.BlockSpec((1, d, tf), lambda i, j: (layer, 0, j)),
                  pl.BlockSpec((1, tf, d), lambda i, j: (layer, j, 0))],
        out_specs=pl.BlockSpec(memory_space=pl.ANY),
        out_shape=jax.ShapeDtypeStruct((m, d), F32),
        scratch_shapes=[pltpu.VMEM((tm, d), F32), pltpu.VMEM((tm, d), BF16),
                        pltpu.SemaphoreType.DMA((tm // norm_rows,)),
                        pltpu.SemaphoreType.DMA((FFN_STORE_CHUNKS,))],
        compiler_params=pltpu.CompilerParams(dimension_semantics=("arbitrary", "arbitrary"),
                                             vmem_limit_bytes=FFN_VMEM_LIMIT),
        name="ffn",
    )(x2, g.reshape(1, d), wg_all, wu_all, wd_all)


def _alibi_slopes(n):
    return jnp.exp2(-8.0 * jnp.arange(1, n + 1, dtype=F32) / n)


def kernel(x, norm1_g, w_in, a_q_g, a_k_g, lambda_q1, lambda_k1, lambda_q2, lambda_k2, a_out_g,
           b_q_g, b_k_g, b_rpb, b_out_g, c_q_g, c_k_g, c_out_g, w_out, norm2_g, w_gate, w_up, w_down):
    batch, seq, d_model = x.shape
    depth = w_in.shape[0]
    n_slots = d_model // HEAD_DIM
    a_heads = n_slots // 8
    b_heads = (n_slots - 2 * a_heads) // 2
    c_heads = n_slots - 2 * a_heads - b_heads
    a_w, b_w, c_w = a_heads * 2 * HEAD_DIM, b_heads * HEAD_DIM, c_heads * HEAD_DIM
    col_a, col_b, col_c = 0, 3 * a_w, 3 * a_w + 3 * b_w
    tn = 512
    norm_tiles = tuple((c // tn, (c + 2 * w) // tn) for c, w in ((col_a, a_w), (col_b, b_w), (col_c, c_w)))
    scale = HEAD_DIM ** -0.5 * LOG2E
    slopes_a = _alibi_slopes(a_heads) * LOG2E
    slopes_c = _alibi_slopes(c_heads) * LOG2E

    x2 = x.reshape(batch * seq, d_model)
    for l in range(depth):
        lam_init = 0.8 - 0.6 * math.exp(-0.3 * l)
        gcols = jnp.concatenate([
            jnp.tile(a_q_g[l] * scale, 2 * a_heads), jnp.tile(a_k_g[l], 2 * a_heads), jnp.ones((a_w,), F32),
            jnp.tile(b_q_g[l] * scale, b_heads), jnp.tile(b_k_g[l], b_heads), jnp.ones((b_w,), F32),
            jnp.tile(c_q_g[l] * scale, c_heads), jnp.tile(c_k_g[l], c_heads), jnp.ones((c_w,), F32),
        ]).astype(F32).reshape(1, -1)
        lamv = jnp.stack([lambda_q1[l], lambda_k1[l], lambda_q2[l], lambda_k2[l]]).astype(F32)

        proj = _inproj(x2, norm1_g[l], w_in, l, gcols, norm_tiles, tn=tn)
        ma = _attn_a(proj, slopes_a, lamv, a_out_g[l], batch=batch, seq=seq, heads=a_heads,
                     col0=col_a, lam_init=lam_init)
        mb = _attn_b(proj, _nbr_bias_table(b_rpb[l]), b_out_g[l], batch=batch, seq=seq,
                     heads=b_heads, col0=col_b)
        mc = _attn_c(proj, slopes_c, c_out_g[l], batch=batch, seq=seq, heads=c_heads, col0=col_c)
        x2 = _outproj(ma, mb, mc, w_out, l, x2)
        x2 = _ffn(x2, norm2_g[l], w_gate, w_up, w_down, l)
    return x2.reshape(batch, seq, d_model)
```

```python
import functools
import math

import jax
import jax.numpy as jnp
from jax import lax
from jax.experimental import pallas as pl
from jax.experimental.pallas import tpu as pltpu

F32 = jnp.float32
BF16 = jnp.bfloat16

HEAD_DIM = 128
GRID_W = 64
NA_ROWS = 8
NA_COLS = 16
DILATED_BRANCHES = ((128, 1), (512, 4), (2048, 16))
RMS_EPS = 1e-6
NEG = -1e30
LOG2E = math.log2(math.e)

VMEM_LIMIT = 56 * 1024 * 1024
FFN_VMEM_LIMIT = 60 * 1024 * 1024
X_RING_SLOTS = 4
NORM_COLS = 512
FFN_STORE_CHUNKS = 4

_NT = (((1,), (1,)), ((), ()))


def _rms(y):
    return y * lax.rsqrt(jnp.mean(y * y, axis=-1, keepdims=True) + RMS_EPS)


def _norm_rows(src, g_ref, dst):
    rows, k = src.shape
    ss = jnp.zeros((rows, 1), F32)
    for c in range(0, k, NORM_COLS):
        blk = src[:, c:c + NORM_COLS]
        ss = ss + jnp.sum(blk * blk, axis=-1, keepdims=True)
    inv = lax.rsqrt(ss * (1.0 / k) + RMS_EPS)
    for c in range(0, k, NORM_COLS):
        cols = slice(c, c + NORM_COLS)
        dst[:, cols] = (src[:, cols] * inv * g_ref[:, cols]).astype(dst.dtype)


def _params(sem):
    return pltpu.CompilerParams(dimension_semantics=sem, vmem_limit_bytes=VMEM_LIMIT)


def _inproj_kernel(x_hbm, g1_ref, w_ref, g_ref, o_ref, h_ref, xs_ref, sem,
                   *, norm_tiles, tm, tn, norm_rows):
    i = pl.program_id(0)
    j = pl.program_id(1)
    n_chunks = tm // norm_rows

    n_slots = xs_ref.shape[0]

    def load_x(r):
        return pltpu.make_async_copy(
            x_hbm.at[pl.ds(pl.multiple_of(i * tm + r * norm_rows, norm_rows), norm_rows), :],
            xs_ref.at[r % n_slots], sem.at[r % n_slots])

    @pl.when(j == 0)
    def _():
        for r in range(n_slots - 1):
            load_x(r).start()

        def norm(r, carry):
            load_x(r).wait()

            @pl.when(r + n_slots - 1 < n_chunks)
            def _():
                load_x(r + n_slots - 1).start()

            rows = pl.ds(pl.multiple_of(r * norm_rows, norm_rows), norm_rows)
            _norm_rows(xs_ref.at[r % n_slots], g1_ref, h_ref.at[rows])
            return carry

        lax.fori_loop(0, n_chunks, norm, 0)

    w = w_ref[0].astype(BF16)
    is_norm = functools.reduce(jnp.logical_or, [(j >= lo) & (j < hi) for lo, hi in norm_tiles])
    acc = jnp.dot(h_ref[...], w, preferred_element_type=F32)
    for c in range(tn // HEAD_DIM):
        sl = slice(c * HEAD_DIM, (c + 1) * HEAD_DIM)
        y = acc[:, sl]
        inv = lax.rsqrt(jnp.mean(y * y, axis=-1, keepdims=True) + RMS_EPS)
        o_ref[:, sl] = (y * (jnp.where(is_norm, inv, 1.0) * g_ref[:, sl])).astype(o_ref.dtype)


def _inproj(x2, g1, w_all, layer, gcols, norm_tiles, tm=1024, tn=512, norm_rows=128):
    m, k = x2.shape
    n = w_all.shape[2]
    return pl.pallas_call(
        functools.partial(_inproj_kernel, norm_tiles=norm_tiles, tm=tm, tn=tn, norm_rows=norm_rows),
        grid=(m // tm, n // tn),
        in_specs=[pl.BlockSpec(memory_space=pl.ANY),
                  pl.BlockSpec((1, k), lambda i, j: (0, 0)),
                  pl.BlockSpec((1, k, tn), lambda i, j: (layer, 0, j)),
                  pl.BlockSpec((1, tn), lambda i, j: (0, j))],
        out_specs=pl.BlockSpec((tm, tn), lambda i, j: (i, j)),
        out_shape=jax.ShapeDtypeStruct((m, n), BF16),
        scratch_shapes=[pltpu.VMEM((tm, k), BF16), pltpu.VMEM((X_RING_SLOTS, norm_rows, k), F32),
                        pltpu.SemaphoreType.DMA((X_RING_SLOTS,))],
        compiler_params=_params(("arbitrary", "arbitrary")),
        name="inproj",
    )(x2, g1.reshape(1, k), w_all, gcols)


def _attn_a_kernel(slopes_ref, lamv_ref, q_ref, k_ref, v_ref, g_ref, o_ref, tb_ref,
                   *, tq, tk, seq, lam_init, group):
    h = pl.program_id(0)
    b = pl.program_id(1)
    nq = seq // tq

    @pl.when(b == 0)
    def _():
        slope = slopes_ref[h]
        d0 = (lax.broadcasted_iota(jnp.int32, (tq, tq), 1)
              - lax.broadcasted_iota(jnp.int32, (tq, tq), 0))
        for cx in range(2 * nq - 1):
            delta = d0 + (cx - (nq - 1)) * tq
            tb_ref[cx] = jnp.abs(delta).astype(F32) * (-slope)

    lv = lamv_ref[...]
    lam = (jnp.exp(jnp.sum(lv[0:1] * lv[1:2], axis=-1, keepdims=True))
           - jnp.exp(jnp.sum(lv[2:3] * lv[3:4], axis=-1, keepdims=True)) + lam_init)
    sub = tk // tq

    def one_block(qi):
        q0 = pl.multiple_of(qi * tq, tq)
        q = q_ref[pl.ds(q0, tq), :]
        m_run = l_run = acc = None
        for c in range(seq // tk):
            bias = jnp.concatenate([tb_ref[(c * sub + t) - qi + (nq - 1)] for t in range(sub)],
                                   axis=1)
            s = jnp.concatenate(
                [lax.dot_general(q[:, m * HEAD_DIM:(m + 1) * HEAD_DIM],
                                 k_ref[c * tk:(c + 1) * tk, m * HEAD_DIM:(m + 1) * HEAD_DIM], _NT,
                                 preferred_element_type=F32) + bias for m in range(2)], axis=0)
            mx = jnp.max(s, axis=-1, keepdims=True)
            m_new = mx if c == 0 else jnp.maximum(m_run, mx)
            p = jnp.exp2(s - m_new)
            l_new = jnp.sum(p, axis=-1, keepdims=True)
            pv = jnp.dot(p.astype(BF16), v_ref[c * tk:(c + 1) * tk, :], preferred_element_type=F32)
            if c == 0:
                l_run, acc = l_new, pv
            else:
                alpha = jnp.exp2(m_run - m_new)
                l_run = alpha * l_run + l_new
                acc = alpha * acc + pv
            m_run = m_new
        o = acc[:tq] * (1.0 / l_run[:tq]) - acc[tq:] * (lam / l_run[tq:])
        o_ref[pl.ds(q0, tq), :] = (_rms(o) * (g_ref[...] * (1.0 - lam_init))).astype(o_ref.dtype)

    def body(i, carry):
        for u in range(group):
            one_block(i * group + u)
        return carry

    lax.fori_loop(0, nq // group, body, 0)


def _attn_a(proj, slopes, lamv, g_out, *, batch, seq, heads, col0, lam_init, tq=256, tk=1024,
            group=2):
    vd = 2 * HEAD_DIM
    nq = seq // tq
    assert nq % group == 0 and tk % tq == 0
    qb, kb, vb = (col0 // vd, col0 // vd + heads, col0 // vd + 2 * heads)
    smem = pl.BlockSpec(memory_space=pltpu.SMEM)
    return pl.pallas_call(
        functools.partial(_attn_a_kernel, tq=tq, tk=tk, seq=seq, lam_init=lam_init, group=group),
        grid=(heads, batch),
        in_specs=[smem,
                  pl.BlockSpec((4, HEAD_DIM), lambda h, b: (0, 0)),
                  pl.BlockSpec((seq, vd), lambda h, b: (b, qb + h)),
                  pl.BlockSpec((seq, vd), lambda h, b: (b, kb + h)),
                  pl.BlockSpec((seq, vd), lambda h, b: (b, vb + h)),
                  pl.BlockSpec((1, vd), lambda h, b: (0, 0))],
        out_specs=pl.BlockSpec((seq, vd), lambda h, b: (b, h)),
        out_shape=jax.ShapeDtypeStruct((batch * seq, heads * vd), BF16),
        scratch_shapes=[pltpu.VMEM((2 * nq - 1, tq, tq), F32)],
        compiler_params=_params(("arbitrary", "arbitrary")),
        name="attn_a",
    )(slopes, lamv, proj, proj, proj, g_out.reshape(1, vd))


NB_QROWS = 4
NB_WROWS = NB_QROWS + NA_ROWS


def _nbr_plan(rows):
    variants, var_of_block = [], []
    for rb in range(rows // NB_QROWS):
        r0 = rb * NB_QROWS
        ks = min(max(r0 - NA_ROWS // 2, 0), rows - NB_WROWS)
        starts = tuple(min(max(r - NA_ROWS // 2, 0), rows - NA_ROWS) - ks
                       for r in range(r0, r0 + NB_QROWS))
        assert all(0 <= st and st + NA_ROWS <= NB_WROWS for st in starts)
        pat = (r0 - ks, starts)
        if pat not in variants:
            variants.append(pat)
        var_of_block.append(variants.index(pat))
    return variants, var_of_block


def _attn_b_kernel(q_ref, k_ref, v_ref, tab_ref, g_ref, o_ref, bias_ref, v1_ref, *, rows, group):
    b = pl.program_id(1)
    variants, var_of_block = _nbr_plan(rows)
    tq = NB_QROWS * GRID_W
    tkw = NB_WROWS * GRID_W

    @pl.when(b == 0)
    def _():
        for vi, (off, starts) in enumerate(variants):
            for qr in range(NB_QROWS):
                pieces = []
                for kr in range(NB_WROWS):
                    if starts[qr] <= kr < starts[qr] + NA_ROWS:
                        pieces.append(tab_ref[0, kr - (off + qr) + NA_ROWS - 1])
                    else:
                        pieces.append(jnp.full((GRID_W, GRID_W), NEG, F32))
                bias_ref[vi, qr * GRID_W:(qr + 1) * GRID_W, :] = jnp.concatenate(pieces, axis=1)

    v1_ref[:, :HEAD_DIM] = v_ref[...]
    v1_ref[:, HEAD_DIM:] = jnp.ones((rows * GRID_W, HEAD_DIM), v1_ref.dtype)

    default = max(set(var_of_block), key=var_of_block.count)

    def one_block(rb):
        vi = default
        for j, var in enumerate(var_of_block):
            if var != default:
                vi = jnp.where(rb == j, var, vi)
        ks = jnp.clip(rb * NB_QROWS - NA_ROWS // 2, 0, rows - NB_WROWS)
        q0 = pl.multiple_of(rb * tq, tq)
        k0 = pl.multiple_of(ks * GRID_W, GRID_W)
        s = lax.dot_general(q_ref[pl.ds(q0, tq), :], k_ref[pl.ds(k0, tkw), :], _NT,
                            preferred_element_type=F32) + bias_ref[vi]
        p = jnp.exp2(s - jnp.max(s, axis=-1, keepdims=True))
        ol = jnp.dot(p.astype(BF16), v1_ref[pl.ds(k0, tkw), :], preferred_element_type=F32)
        o = ol[:, :HEAD_DIM] * (1.0 / ol[:, HEAD_DIM:HEAD_DIM + 1])
        o_ref[pl.ds(q0, tq), :] = (_rms(o) * g_ref[...]).astype(o_ref.dtype)

    def body(i, carry):
        for u in range(group):
            one_block(i * group + u)
        return carry

    lax.fori_loop(0, rows // NB_QROWS // group, body, 0)


def _attn_b(proj, tab, g_out, *, batch, seq, heads, col0, group=16):
    rows = seq // GRID_W
    assert rows % (NB_QROWS * group) == 0 and rows >= NB_WROWS
    qb = col0 // HEAD_DIM
    kb, vb = qb + heads, qb + 2 * heads
    nvar = len(_nbr_plan(rows)[0])
    return pl.pallas_call(
        functools.partial(_attn_b_kernel, rows=rows, group=group),
        grid=(heads, batch),
        in_specs=[pl.BlockSpec((seq, HEAD_DIM), lambda h, b: (b, qb + h)),
                  pl.BlockSpec((seq, HEAD_DIM), lambda h, b: (b, kb + h)),
                  pl.BlockSpec((seq, HEAD_DIM), lambda h, b: (b, vb + h)),
                  pl.BlockSpec((1, 2 * NA_ROWS - 1, GRID_W, GRID_W), lambda h, b: (h, 0, 0, 0)),
                  pl.BlockSpec((1, HEAD_DIM), lambda h, b: (0, 0))],
        out_specs=pl.BlockSpec((seq, HEAD_DIM), lambda h, b: (b, h)),
        out_shape=jax.ShapeDtypeStruct((batch * seq, heads * HEAD_DIM), BF16),
        scratch_shapes=[pltpu.VMEM((nvar, NB_QROWS * GRID_W, NB_WROWS * GRID_W), F32),
                        pltpu.VMEM((seq, 2 * HEAD_DIM), BF16)],
        compiler_params=_params(("arbitrary", "arbitrary")),
        name="attn_b",
    )(proj, proj, proj, tab, g_out.reshape(1, HEAD_DIM))


def _nbr_bias_table(rpb):
    cols = jnp.arange(GRID_W)
    c_start = jnp.clip(cols - NA_COLS // 2, 0, GRID_W - NA_COLS)
    col_mask = (cols[None, :] >= c_start[:, None]) & (cols[None, :] < c_start[:, None] + NA_COLS)
    dc = cols[None, :] - cols[:, None] + NA_COLS - 1
    t = jnp.full(rpb.shape[:2] + (GRID_W, GRID_W), NEG, F32)
    for kc in range(2 * NA_COLS - 1):
        t = jnp.where((dc == kc) & col_mask, rpb[:, :, kc, None, None].astype(F32) * LOG2E, t)
    return t


def _attn_c_kernel(slopes_ref, q_ref, k_ref, v_ref, g_ref, o_ref, dist_ref, logm_ref, tb_ref, v1_ref,
                   *, tq, seq, nwin, group):
    h = pl.program_id(0)
    b = pl.program_id(1)
    nq = seq // tq
    reach = (nwin - 1) // 2
    ncx = 2 * (nwin - 1) + 1

    @pl.when((h == 0) & (b == 0))
    def _():
        d0 = (lax.broadcasted_iota(jnp.int32, (tq, tq), 1)
              - lax.broadcasted_iota(jnp.int32, (tq, tq), 0))
        for cx in range(ncx):
            delta = d0 + (cx - (nwin - 1)) * tq
            ad = jnp.abs(delta)
            mult = jnp.zeros((tq, tq), F32)
            for window, dil in DILATED_BRANCHES:
                radius = window // (2 * dil)
                hit = ((delta & (dil - 1)) == 0) & (ad <= radius * dil)
                mult = mult + hit.astype(F32)
            dist_ref[cx] = ad.astype(F32)
            logm_ref[cx] = jnp.where(mult > 0, jnp.log2(jnp.maximum(mult, 1.0)), NEG)

    @pl.when(b == 0)
    def _():
        slope = slopes_ref[h]
        for cx in range(ncx):
            tb_ref[cx] = dist_ref[cx] * (-slope) + logm_ref[cx]

    v1_ref[:, :HEAD_DIM] = v_ref[...]
    v1_ref[:, HEAD_DIM:] = jnp.ones((seq, HEAD_DIM), v1_ref.dtype)

    def one_block(qi):
        ks = jnp.clip(qi - reach, 0, nq - nwin)
        k0 = pl.multiple_of(ks * tq, tq)
        q0 = pl.multiple_of(qi * tq, tq)
        bias = jnp.concatenate([tb_ref[ks + t - qi + (nwin - 1)] for t in range(nwin)], axis=1)
        s = lax.dot_general(q_ref[pl.ds(q0, tq), :], k_ref[pl.ds(k0, nwin * tq), :], _NT,
                            preferred_element_type=F32) + bias
        p = jnp.exp2(s - jnp.max(s, axis=-1, keepdims=True))
        ol = jnp.dot(p.astype(BF16), v1_ref[pl.ds(k0, nwin * tq), :], preferred_element_type=F32)
        o = ol[:, :HEAD_DIM] * (1.0 / ol[:, HEAD_DIM:HEAD_DIM + 1])
        o_ref[pl.ds(q0, tq), :] = (_rms(o) * g_ref[...]).astype(o_ref.dtype)

    def body(i, carry):
        for u in range(group):
            one_block(i * group + u)
        return carry

    lax.fori_loop(0, nq // group, body, 0)


def _attn_c(proj, slopes, g_out, *, batch, seq, heads, col0, tq=256, group=16):
    for _, dil in DILATED_BRANCHES:
        assert dil & (dil - 1) == 0
    span = max((w // (2 * d)) * d for w, d in DILATED_BRANCHES)
    nwin = 2 * (span // tq) + 1
    nq = seq // tq
    assert span % tq == 0 and nq >= nwin and nq % group == 0
    qb = col0 // HEAD_DIM
    kb, vb = qb + heads, qb + 2 * heads
    ncx = 2 * (nwin - 1) + 1
    smem = pl.BlockSpec(memory_space=pltpu.SMEM)
    return pl.pallas_call(
        functools.partial(_attn_c_kernel, tq=tq, seq=seq, nwin=nwin, group=group),
        grid=(heads, batch),
        in_specs=[smem,
                  pl.BlockSpec((seq, HEAD_DIM), lambda h, b: (b, qb + h)),
                  pl.BlockSpec((seq, HEAD_DIM), lambda h, b: (b, kb + h)),
                  pl.BlockSpec((seq, HEAD_DIM), lambda h, b: (b, vb + h)),
                  pl.BlockSpec((1, HEAD_DIM), lambda h, b: (0, 0))],
        out_specs=pl.BlockSpec((seq, HEAD_DIM), lambda h, b: (b, h)),
        out_shape=jax.ShapeDtypeStruct((batch * seq, heads * HEAD_DIM), BF16),
        scratch_shapes=[pltpu.VMEM((ncx, tq, tq), F32), pltpu.VMEM((ncx, tq, tq), F32),
                        pltpu.VMEM((ncx, tq, tq), F32), pltpu.VMEM((seq, 2 * HEAD_DIM), BF16)],
        compiler_params=_params(("arbitrary", "arbitrary")),
        name="attn_c",
    )(slopes, proj, proj, proj, g_out.reshape(1, HEAD_DIM))


def _outproj_kernel(a_ref, b_ref, c_ref, w_ref, x_ref, o_ref, *, splits):
    acc = x_ref[...]
    row = 0
    for part, width in zip((a_ref, b_ref, c_ref), splits):
        acc = acc + jnp.dot(part[...], w_ref[0, row:row + width, :].astype(BF16),
                            preferred_element_type=F32)
        row += width
    o_ref[...] = acc


def _outproj(ma, mb, mc, w_all, layer, x2, tm=1024, tn=512):
    m, n = x2.shape
    splits = (ma.shape[1], mb.shape[1], mc.shape[1])
    k = sum(splits)
    return pl.pallas_call(
        functools.partial(_outproj_kernel, splits=splits),
        grid=(m // tm, n // tn),
        in_specs=[pl.BlockSpec((tm, splits[0]), lambda i, j: (i, 0)),
                  pl.BlockSpec((tm, splits[1]), lambda i, j: (i, 0)),
                  pl.BlockSpec((tm, splits[2]), lambda i, j: (i, 0)),
                  pl.BlockSpec((1, k, tn), lambda i, j: (layer, 0, j)),
                  pl.BlockSpec((tm, tn), lambda i, j: (i, j))],
        out_specs=pl.BlockSpec((tm, tn), lambda i, j: (i, j)),
        out_shape=jax.ShapeDtypeStruct((m, n), F32),
        compiler_params=_params(("arbitrary", "arbitrary")),
        name="outproj",
    )(ma, mb, mc, w_all, x2)


def _ffn_kernel(x_hbm, g_ref, wg_ref, wu_ref, wd_ref, o_hbm, acc_ref, h_ref, load_sem, store_sem,
                *, tm, norm_rows):
    i = pl.program_id(0)
    j = pl.program_id(1)
    n_chunks = tm // norm_rows
    store_rows = tm // FFN_STORE_CHUNKS

    def load_x(r):
        return pltpu.make_async_copy(
            x_hbm.at[pl.ds(pl.multiple_of(i * tm + r * norm_rows, norm_rows), norm_rows), :],
            acc_ref.at[pl.ds(pl.multiple_of(r * norm_rows, norm_rows), norm_rows), :],
            load_sem.at[r])

    def store_o(r):
        return pltpu.make_async_copy(
            acc_ref.at[pl.ds(r * store_rows, store_rows), :],
            o_hbm.at[pl.ds(pl.multiple_of(i * tm + r * store_rows, store_rows), store_rows), :],
            store_sem.at[r])

    @pl.when(j == 0)
    def _():
        for r in range(n_chunks):
            load_x(r).start()

        def norm(r, carry):
            load_x(r).wait()
            rows = pl.ds(pl.multiple_of(r * norm_rows, norm_rows), norm_rows)
            _norm_rows(acc_ref.at[rows], g_ref, h_ref.at[rows])
            return carry

        lax.fori_loop(0, n_chunks, norm, 0)

    def step(write_back):
        h = h_ref[...]
        gate = jnp.dot(h, wg_ref[0].astype(BF16), preferred_element_type=F32)
        up = jnp.dot(h, wu_ref[0].astype(BF16), preferred_element_type=F32)
        act = (gate * jax.nn.sigmoid(gate) * up).astype(BF16)
        wd = wd_ref[0].astype(BF16)
        if not write_back:
            acc_ref[...] += jnp.dot(act, wd, preferred_element_type=F32)
            return
        for r in range(FFN_STORE_CHUNKS):
            rows = slice(r * store_rows, (r + 1) * store_rows)
            acc_ref[rows, :] += jnp.dot(act[rows, :], wd, preferred_element_type=F32)
            store_o(r).start()
        for r in range(FFN_STORE_CHUNKS):
            store_o(r).wait()

    is_last = j == pl.num_programs(1) - 1
    pl.when(jnp.logical_not(is_last))(functools.partial(step, False))
    pl.when(is_last)(functools.partial(step, True))


def _ffn(x2, g, wg_all, wu_all, wd_all, layer, tm=1024, tf=256, norm_rows=128):
    m, d = x2.shape
    f = wg_all.shape[2]
    return pl.pallas_call(
        functools.partial(_ffn_kernel, tm=tm, norm_rows=norm_rows),
        grid=(m // tm, f // tf),
        in_specs=[pl.BlockSpec(memory_space=pl.ANY),
                  pl.BlockSpec((1, d), lambda i, j: (0, 0)),
                  pl.BlockSpec((1, d, tf), lambda i, j: (layer, 0, j)),
                  pl.BlockSpec((1, d, tf), lambda i, j: (layer, 0, j)),
                  pl.BlockSpec((1, tf, d), lambda i, j: (layer, j, 0))],
        out_specs=pl.BlockSpec(memory_space=pl.ANY),
        out_shape=jax.ShapeDtypeStruct((m, d), F32),
        scratch_shapes=[pltpu.VMEM((tm, d), F32), pltpu.VMEM((tm, d), BF16),
                        pltpu.SemaphoreType.DMA((tm // norm_rows,)),
                        pltpu.SemaphoreType.DMA((FFN_STORE_CHUNKS,))],
        compiler_params=pltpu.CompilerParams(dimension_semantics=("arbitrary", "arbitrary"),
                                             vmem_limit_bytes=FFN_VMEM_LIMIT),
        name="ffn",
    )(x2, g.reshape(1, d), wg_all, wu_all, wd_all)


def _alibi_slopes(n):
    return jnp.exp2(-8.0 * jnp.arange(1, n + 1, dtype=F32) / n)


def kernel(x, norm1_g, w_in, a_q_g, a_k_g, lambda_q1, lambda_k1, lambda_q2, lambda_k2, a_out_g,
           b_q_g, b_k_g, b_rpb, b_out_g, c_q_g, c_k_g, c_out_g, w_out, norm2_g, w_gate, w_up, w_down):
    batch, seq, d_model = x.shape
    depth = w_in.shape[0]
    n_slots = d_model // HEAD_DIM
    a_heads = n_slots // 8
    b_heads = (n_slots - 2 * a_heads) // 2
    c_heads = n_slots - 2 * a_heads - b_heads
    a_w, b_w, c_w = a_heads * 2 * HEAD_DIM, b_heads * HEAD_DIM, c_heads * HEAD_DIM
    col_a, col_b, col_c = 0, 3 * a_w, 3 * a_w + 3 * b_w
    tn = 512
    norm_tiles = tuple((c // tn, (c + 2 * w) // tn) for c, w in ((col_a, a_w), (col_b, b_w), (col_c, c_w)))
    scale = HEAD_DIM ** -0.5 * LOG2E
    slopes_a = _alibi_slopes(a_heads) * LOG2E
    slopes_c = _alibi_slopes(c_heads) * LOG2E

    x2 = x.reshape(batch * seq, d_model)
    for l in range(depth):
        lam_init = 0.8 - 0.6 * math.exp(-0.3 * l)
        gcols = jnp.concatenate([
            jnp.tile(a_q_g[l] * scale, 2 * a_heads), jnp.tile(a_k_g[l], 2 * a_heads), jnp.ones((a_w,), F32),
            jnp.tile(b_q_g[l] * scale, b_heads), jnp.tile(b_k_g[l], b_heads), jnp.ones((b_w,), F32),
            jnp.tile(c_q_g[l] * scale, c_heads), jnp.tile(c_k_g[l], c_heads), jnp.ones((c_w,), F32),
        ]).astype(F32).reshape(1, -1)
        lamv = jnp.stack([lambda_q1[l], lambda_k1[l], lambda_q2[l], lambda_k2[l]]).astype(F32)

        proj = _inproj(x2, norm1_g[l], w_in, l, gcols, norm_tiles, tn=tn)
        ma = _attn_a(proj, slopes_a, lamv, a_out_g[l], batch=batch, seq=seq, heads=a_heads,
                     col0=col_a, lam_init=lam_init)
        mb = _attn_b(proj, _nbr_bias_table(b_rpb[l]), b_out_g[l], batch=batch, seq=seq,
                     heads=b_heads, col0=col_b)
        mc = _attn_c(proj, slopes_c, c_out_g[l], batch=batch, seq=seq, heads=c_heads, col0=col_c)
        x2 = _outproj(ma, mb, mc, w_out, l, x2)
        x2 = _ffn(x2, norm2_g[l], w_gate, w_up, w_down, l)
    return x2.reshape(batch, seq, d_model)
```

```python
import functools
import math

import jax
import jax.numpy as jnp
from jax import lax
from jax.experimental import pallas as pl
from jax.experimental.pallas import tpu as pltpu

F32 = jnp.float32
BF16 = jnp.bfloat16

HEAD_DIM = 128
GRID_W = 64
NA_ROWS = 8
NA_COLS = 16
DILATED_BRANCHES = ((128, 1), (512, 4), (2048, 16))
RMS_EPS = 1e-6
NEG = -1e30
LOG2E = math.log2(math.e)

VMEM_LIMIT = 56 * 1024 * 1024
FFN_VMEM_LIMIT = 60 * 1024 * 1024
X_RING_SLOTS = 4
NORM_COLS = 512
FFN_STORE_CHUNKS = 4

_NT = (((1,), (1,)), ((), ()))


def _rms(y):
    return y * lax.rsqrt(jnp.mean(y * y, axis=-1, keepdims=True) + RMS_EPS)


def _norm_rows(src, g_ref, dst):
    rows, k = src.shape
    ss = jnp.zeros((rows, 1), F32)
    for c in range(0, k, NORM_COLS):
        blk = src[:, c:c + NORM_COLS]
        ss = ss + jnp.sum(blk * blk, axis=-1, keepdims=True)
    inv = lax.rsqrt(ss * (1.0 / k) + RMS_EPS)
    for c in range(0, k, NORM_COLS):
        cols = slice(c, c + NORM_COLS)
        dst[:, cols] = (src[:, cols] * inv * g_ref[:, cols]).astype(dst.dtype)


def _params(sem):
    return pltpu.CompilerParams(dimension_semantics=sem, vmem_limit_bytes=VMEM_LIMIT)


def _inproj_kernel(x_hbm, g1_ref, w_ref, g_ref, o_ref, h_ref, xs_ref, sem,
                   *, norm_tiles, tm, tn, norm_rows):
    i = pl.program_id(0)
    j = pl.program_id(1)
    n_chunks = tm // norm_rows

    n_slots = xs_ref.shape[0]

    def load_x(r):
        return pltpu.make_async_copy(
            x_hbm.at[pl.ds(pl.multiple_of(i * tm + r * norm_rows, norm_rows), norm_rows), :],
            xs_ref.at[r % n_slots], sem.at[r % n_slots])

    @pl.when(j == 0)
    def _():
        for r in range(n_slots - 1):
            load_x(r).start()

        def norm(r, carry):
            load_x(r).wait()

            @pl.when(r + n_slots - 1 < n_chunks)
            def _():
                load_x(r + n_slots - 1).start()

            rows = pl.ds(pl.multiple_of(r * norm_rows, norm_rows), norm_rows)
            _norm_rows(xs_ref.at[r % n_slots], g1_ref, h_ref.at[rows])
            return carry

        lax.fori_loop(0, n_chunks, norm, 0)

    w = w_ref[0].astype(BF16)
    is_norm = functools.reduce(jnp.logical_or, [(j >= lo) & (j < hi) for lo, hi in norm_tiles])
    acc = jnp.dot(h_ref[...], w, preferred_element_type=F32)
    for c in range(tn // HEAD_DIM):
        sl = slice(c * HEAD_DIM, (c + 1) * HEAD_DIM)
        y = acc[:, sl]
        inv = lax.rsqrt(jnp.mean(y * y, axis=-1, keepdims=True) + RMS_EPS)
        o_ref[:, sl] = (y * (jnp.where(is_norm, inv, 1.0) * g_ref[:, sl])).astype(o_ref.dtype)


def _inproj(x2, g1, w_all, layer, gcols, norm_tiles, tm=2048, tn=512, norm_rows=128):
    m, k = x2.shape
    n = w_all.shape[2]
    return pl.pallas_call(
        functools.partial(_inproj_kernel, norm_tiles=norm_tiles, tm=tm, tn=tn, norm_rows=norm_rows),
        grid=(m // tm, n // tn),
        in_specs=[pl.BlockSpec(memory_space=pl.ANY),
                  pl.BlockSpec((1, k), lambda i, j: (0, 0)),
                  pl.BlockSpec((1, k, tn), lambda i, j: (layer, 0, j)),
                  pl.BlockSpec((1, tn), lambda i, j: (0, j))],
        out_specs=pl.BlockSpec((tm, tn), lambda i, j: (i, j)),
        out_shape=jax.ShapeDtypeStruct((m, n), BF16),
        scratch_shapes=[pltpu.VMEM((tm, k), BF16), pltpu.VMEM((X_RING_SLOTS, norm_rows, k), F32),
                        pltpu.SemaphoreType.DMA((X_RING_SLOTS,))],
        compiler_params=_params(("arbitrary", "arbitrary")),
        name="inproj",
    )(x2, g1.reshape(1, k), w_all, gcols)


def _attn_a_kernel(slopes_ref, lamv_ref, q_ref, k_ref, v_ref, g_ref, o_ref, tb_ref,
                   *, tq, tk, seq, lam_init, group):
    h = pl.program_id(0)
    b = pl.program_id(1)
    nq = seq // tq

    @pl.when(b == 0)
    def _():
        slope = slopes_ref[h]
        d0 = (lax.broadcasted_iota(jnp.int32, (tq, tq), 1)
              - lax.broadcasted_iota(jnp.int32, (tq, tq), 0))
        for cx in range(2 * nq - 1):
            delta = d0 + (cx - (nq - 1)) * tq
            tb_ref[cx] = jnp.abs(delta).astype(F32) * (-slope)

    lv = lamv_ref[...]
    lam = (jnp.exp(jnp.sum(lv[0:1] * lv[1:2], axis=-1, keepdims=True))
           - jnp.exp(jnp.sum(lv[2:3] * lv[3:4], axis=-1, keepdims=True)) + lam_init)
    sub = tk // tq

    def one_block(qi):
        q0 = pl.multiple_of(qi * tq, tq)
        q = q_ref[pl.ds(q0, tq), :]
        m_run = l_run = acc = None
        for c in range(seq // tk):
            bias = jnp.concatenate([tb_ref[(c * sub + t) - qi + (nq - 1)] for t in range(sub)],
                                   axis=1)
            s = jnp.concatenate(
                [lax.dot_general(q[:, m * HEAD_DIM:(m + 1) * HEAD_DIM],
                                 k_ref[c * tk:(c + 1) * tk, m * HEAD_DIM:(m + 1) * HEAD_DIM], _NT,
                                 preferred_element_type=F32) + bias for m in range(2)], axis=0)
            mx = jnp.max(s, axis=-1, keepdims=True)
            m_new = mx if c == 0 else jnp.maximum(m_run, mx)
            p = jnp.exp2(s - m_new)
            l_new = jnp.sum(p, axis=-1, keepdims=True)
            pv = jnp.dot(p.astype(BF16), v_ref[c * tk:(c + 1) * tk, :], preferred_element_type=F32)
            if c == 0:
                l_run, acc = l_new, pv
            else:
                alpha = jnp.exp2(m_run - m_new)
                l_run = alpha * l_run + l_new
                acc = alpha * acc + pv
            m_run = m_new
        o = acc[:tq] * (1.0 / l_run[:tq]) - acc[tq:] * (lam / l_run[tq:])
        o_ref[pl.ds(q0, tq), :] = (_rms(o) * (g_ref[...] * (1.0 - lam_init))).astype(o_ref.dtype)

    def body(i, carry):
        for u in range(group):
            one_block(i * group + u)
        return carry

    lax.fori_loop(0, nq // group, body, 0)


def _attn_a(proj, slopes, lamv, g_out, *, batch, seq, heads, col0, lam_init, tq=256, tk=1024,
            group=4):
    vd = 2 * HEAD_DIM
    nq = seq // tq
    assert nq % group == 0 and tk % tq == 0
    qb, kb, vb = (col0 // vd, col0 // vd + heads, col0 // vd + 2 * heads)
    smem = pl.BlockSpec(memory_space=pltpu.SMEM)
    return pl.pallas_call(
        functools.partial(_attn_a_kernel, tq=tq, tk=tk, seq=seq, lam_init=lam_init, group=group),
        grid=(heads, batch),
        in_specs=[smem,
                  pl.BlockSpec((4, HEAD_DIM), lambda h, b: (0, 0)),
                  pl.BlockSpec((seq, vd), lambda h, b: (b, qb + h)),
                  pl.BlockSpec((seq, vd), lambda h, b: (b, kb + h)),
                  pl.BlockSpec((seq, vd), lambda h, b: (b, vb + h)),
                  pl.BlockSpec((1, vd), lambda h, b: (0, 0))],
        out_specs=pl.BlockSpec((seq, vd), lambda h, b: (b, h)),
        out_shape=jax.ShapeDtypeStruct((batch * seq, heads * vd), BF16),
        scratch_shapes=[pltpu.VMEM((2 * nq - 1, tq, tq), F32)],
        compiler_params=_params(("arbitrary", "arbitrary")),
        name="attn_a",
    )(slopes, lamv, proj, proj, proj, g_out.reshape(1, vd))


NB_QROWS = 4
NB_WROWS = NB_QROWS + NA_ROWS


def _nbr_plan(rows):
    variants, var_of_block = [], []
    for rb in range(rows // NB_QROWS):
        r0 = rb * NB_QROWS
        ks = min(max(r0 - NA_ROWS // 2, 0), rows - NB_WROWS)
        starts = tuple(min(max(r - NA_ROWS // 2, 0), rows - NA_ROWS) - ks
                       for r in range(r0, r0 + NB_QROWS))
        assert all(0 <= st and st + NA_ROWS <= NB_WROWS for st in starts)
        pat = (r0 - ks, starts)
        if pat not in variants:
            variants.append(pat)
        var_of_block.append(variants.index(pat))
    return variants, var_of_block


def _attn_b_kernel(q_ref, k_ref, v_ref, tab_ref, g_ref, o_ref, bias_ref, v1_ref, *, rows, group):
    b = pl.program_id(1)
    variants, var_of_block = _nbr_plan(rows)
    tq = NB_QROWS * GRID_W
    tkw = NB_WROWS * GRID_W

    @pl.when(b == 0)
    def _():
        for vi, (off, starts) in enumerate(variants):
            for qr in range(NB_QROWS):
                pieces = []
                for kr in range(NB_WROWS):
                    if starts[qr] <= kr < starts[qr] + NA_ROWS:
                        pieces.append(tab_ref[0, kr - (off + qr) + NA_ROWS - 1])
                    else:
                        pieces.append(jnp.full((GRID_W, GRID_W), NEG, F32))
                bias_ref[vi, qr * GRID_W:(qr + 1) * GRID_W, :] = jnp.concatenate(pieces, axis=1)

    v1_ref[:, :HEAD_DIM] = v_ref[...]
    v1_ref[:, HEAD_DIM:] = jnp.ones((rows * GRID_W, HEAD_DIM), v1_ref.dtype)

    default = max(set(var_of_block), key=var_of_block.count)

    def one_block(rb):
        vi = default
        for j, var in enumerate(var_of_block):
            if var != default:
                vi = jnp.where(rb == j, var, vi)
        ks = jnp.clip(rb * NB_QROWS - NA_ROWS // 2, 0, rows - NB_WROWS)
        q0 = pl.multiple_of(rb * tq, tq)
        k0 = pl.multiple_of(ks * GRID_W, GRID_W)
        s = lax.dot_general(q_ref[pl.ds(q0, tq), :], k_ref[pl.ds(k0, tkw), :], _NT,
                            preferred_element_type=F32) + bias_ref[vi]
        p = jnp.exp2(s - jnp.max(s, axis=-1, keepdims=True))
        ol = jnp.dot(p.astype(BF16), v1_ref[pl.ds(k0, tkw), :], preferred_element_type=F32)
        o = ol[:, :HEAD_DIM] * (1.0 / ol[:, HEAD_DIM:HEAD_DIM + 1])
        o_ref[pl.ds(q0, tq), :] = (_rms(o) * g_ref[...]).astype(o_ref.dtype)

    def body(i, carry):
        for u in range(group):
            one_block(i * group + u)
        return carry

    lax.fori_loop(0, rows // NB_QROWS // group, body, 0)


def _attn_b(proj, tab, g_out, *, batch, seq, heads, col0, group=16):
    rows = seq // GRID_W
    assert rows % (NB_QROWS * group) == 0 and rows >= NB_WROWS
    qb = col0 // HEAD_DIM
    kb, vb = qb + heads, qb + 2 * heads
    nvar = len(_nbr_plan(rows)[0])
    return pl.pallas_call(
        functools.partial(_attn_b_kernel, rows=rows, group=group),
        grid=(heads, batch),
        in_specs=[pl.BlockSpec((seq, HEAD_DIM), lambda h, b: (b, qb + h)),
                  pl.BlockSpec((seq, HEAD_DIM), lambda h, b: (b, kb + h)),
                  pl.BlockSpec((seq, HEAD_DIM), lambda h, b: (b, vb + h)),
                  pl.BlockSpec((1, 2 * NA_ROWS - 1, GRID_W, GRID_W), lambda h, b: (h, 0, 0, 0)),
                  pl.BlockSpec((1, HEAD_DIM), lambda h, b: (0, 0))],
        out_specs=pl.BlockSpec((seq, HEAD_DIM), lambda h, b: (b, h)),
        out_shape=jax.ShapeDtypeStruct((batch * seq, heads * HEAD_DIM), BF16),
        scratch_shapes=[pltpu.VMEM((nvar, NB_QROWS * GRID_W, NB_WROWS * GRID_W), F32),
                        pltpu.VMEM((seq, 2 * HEAD_DIM), BF16)],
        compiler_params=_params(("arbitrary", "arbitrary")),
        name="attn_b",
    )(proj, proj, proj, tab, g_out.reshape(1, HEAD_DIM))


def _nbr_bias_table(rpb):
    cols = jnp.arange(GRID_W)
    c_start = jnp.clip(cols - NA_COLS // 2, 0, GRID_W - NA_COLS)
    col_mask = (cols[None, :] >= c_start[:, None]) & (cols[None, :] < c_start[:, None] + NA_COLS)
    dc = cols[None, :] - cols[:, None] + NA_COLS - 1
    t = jnp.full(rpb.shape[:2] + (GRID_W, GRID_W), NEG, F32)
    for kc in range(2 * NA_COLS - 1):
        t = jnp.where((dc == kc) & col_mask, rpb[:, :, kc, None, None].astype(F32) * LOG2E, t)
    return t


def _attn_c_kernel(slopes_ref, q_ref, k_ref, v_ref, g_ref, o_ref, dist_ref, logm_ref, tb_ref, v1_ref,
                   *, tq, seq, nwin, group):
    h = pl.program_id(0)
    b = pl.program_id(1)
    nq = seq // tq
    reach = (nwin - 1) // 2
    ncx = 2 * (nwin - 1) + 1

    @pl.when((h == 0) & (b == 0))
    def _():
        d0 = (lax.broadcasted_iota(jnp.int32, (tq, tq), 1)
              - lax.broadcasted_iota(jnp.int32, (tq, tq), 0))
        for cx in range(ncx):
            delta = d0 + (cx - (nwin - 1)) * tq
            ad = jnp.abs(delta)
            mult = jnp.zeros((tq, tq), F32)
            for window, dil in DILATED_BRANCHES:
                radius = window // (2 * dil)
                hit = ((delta & (dil - 1)) == 0) & (ad <= radius * dil)
                mult = mult + hit.astype(F32)
            dist_ref[cx] = ad.astype(F32)
            logm_ref[cx] = jnp.where(mult > 0, jnp.log2(jnp.maximum(mult, 1.0)), NEG)

    @pl.when(b == 0)
    def _():
        slope = slopes_ref[h]
        for cx in range(ncx):
            tb_ref[cx] = dist_ref[cx] * (-slope) + logm_ref[cx]

    v1_ref[:, :HEAD_DIM] = v_ref[...]
    v1_ref[:, HEAD_DIM:] = jnp.ones((seq, HEAD_DIM), v1_ref.dtype)

    def one_block(qi):
        ks = jnp.clip(qi - reach, 0, nq - nwin)
        k0 = pl.multiple_of(ks * tq, tq)
        q0 = pl.multiple_of(qi * tq, tq)
        bias = jnp.concatenate([tb_ref[ks + t - qi + (nwin - 1)] for t in range(nwin)], axis=1)
        s = lax.dot_general(q_ref[pl.ds(q0, tq), :], k_ref[pl.ds(k0, nwin * tq), :], _NT,
                            preferred_element_type=F32) + bias
        p = jnp.exp2(s - jnp.max(s, axis=-1, keepdims=True))
        ol = jnp.dot(p.astype(BF16), v1_ref[pl.ds(k0, nwin * tq), :], preferred_element_type=F32)
        o = ol[:, :HEAD_DIM] * (1.0 / ol[:, HEAD_DIM:HEAD_DIM + 1])
        o_ref[pl.ds(q0, tq), :] = (_rms(o) * g_ref[...]).astype(o_ref.dtype)

    def body(i, carry):
        for u in range(group):
            one_block(i * group + u)
        return carry

    lax.fori_loop(0, nq // group, body, 0)


def _attn_c(proj, slopes, g_out, *, batch, seq, heads, col0, tq=256, group=16):
    for _, dil in DILATED_BRANCHES:
        assert dil & (dil - 1) == 0
    span = max((w // (2 * d)) * d for w, d in DILATED_BRANCHES)
    nwin = 2 * (span // tq) + 1
    nq = seq // tq
    assert span % tq == 0 and nq >= nwin and nq % group == 0
    qb = col0 // HEAD_DIM
    kb, vb = qb + heads, qb + 2 * heads
    ncx = 2 * (nwin - 1) + 1
    smem = pl.BlockSpec(memory_space=pltpu.SMEM)
    return pl.pallas_call(
        functools.partial(_attn_c_kernel, tq=tq, seq=seq, nwin=nwin, group=group),
        grid=(heads, batch),
        in_specs=[smem,
                  pl.BlockSpec((seq, HEAD_DIM), lambda h, b: (b, qb + h)),
                  pl.BlockSpec((seq, HEAD_DIM), lambda h, b: (b, kb + h)),
                  pl.BlockSpec((seq, HEAD_DIM), lambda h, b: (b, vb + h)),
                  pl.BlockSpec((1, HEAD_DIM), lambda h, b: (0, 0))],
        out_specs=pl.BlockSpec((seq, HEAD_DIM), lambda h, b: (b, h)),
        out_shape=jax.ShapeDtypeStruct((batch * seq, heads * HEAD_DIM), BF16),
        scratch_shapes=[pltpu.VMEM((ncx, tq, tq), F32), pltpu.VMEM((ncx, tq, tq), F32),
                        pltpu.VMEM((ncx, tq, tq), F32), pltpu.VMEM((seq, 2 * HEAD_DIM), BF16)],
        compiler_params=_params(("arbitrary", "arbitrary")),
        name="attn_c",
    )(slopes, proj, proj, proj, g_out.reshape(1, HEAD_DIM))


def _outproj_kernel(a_ref, b_ref, c_ref, w_ref, x_ref, o_ref, *, splits):
    acc = x_ref[...]
    row = 0
    for part, width in zip((a_ref, b_ref, c_ref), splits):
        acc = acc + jnp.dot(part[...], w_ref[0, row:row + width, :].astype(BF16),
                            preferred_element_type=F32)
        row += width
    o_ref[...] = acc


def _outproj(ma, mb, mc, w_all, layer, x2, tm=1024, tn=512):
    m, n = x2.shape
    splits = (ma.shape[1], mb.shape[1], mc.shape[1])
    k = sum(splits)
    return pl.pallas_call(
        functools.partial(_outproj_kernel, splits=splits),
        grid=(m // tm, n // tn),
        in_specs=[pl.BlockSpec((tm, splits[0]), lambda i, j: (i, 0)),
                  pl.BlockSpec((tm, splits[1]), lambda i, j: (i, 0)),
                  pl.BlockSpec((tm, splits[2]), lambda i, j: (i, 0)),
                  pl.BlockSpec((1, k, tn), lambda i, j: (layer, 0, j)),
                  pl.BlockSpec((tm, tn), lambda i, j: (i, j))],
        out_specs=pl.BlockSpec((tm, tn), lambda i, j: (i, j)),
        out_shape=jax.ShapeDtypeStruct((m, n), F32),
        compiler_params=_params(("arbitrary", "arbitrary")),
        name="outproj",
    )(ma, mb, mc, w_all, x2)


def _ffn_kernel(x_hbm, g_ref, wg_ref, wu_ref, wd_ref, o_hbm, acc_ref, h_ref, load_sem, store_sem,
                *, tm, norm_rows):
    i = pl.program_id(0)
    j = pl.program_id(1)
    n_chunks = tm // norm_rows
    store_rows = tm // FFN_STORE_CHUNKS

    def load_x(r):
        return pltpu.make_async_copy(
            x_hbm.at[pl.ds(pl.multiple_of(i * tm + r * norm_rows, norm_rows), norm_rows), :],
            acc_ref.at[pl.ds(pl.multiple_of(r * norm_rows, norm_rows), norm_rows), :],
            load_sem.at[r])

    def store_o(r):
        return pltpu.make_async_copy(
            acc_ref.at[pl.ds(r * store_rows, store_rows), :],
            o_hbm.at[pl.ds(pl.multiple_of(i * tm + r * store_rows, store_rows), store_rows), :],
            store_sem.at[r])

    @pl.when(j == 0)
    def _():
        for r in range(n_chunks):
            load_x(r).start()

        def norm(r, carry):
            load_x(r).wait()
            rows = pl.ds(pl.multiple_of(r * norm_rows, norm_rows), norm_rows)
            _norm_rows(acc_ref.at[rows], g_ref, h_ref.at[rows])
            return carry

        lax.fori_loop(0, n_chunks, norm, 0)

    def step(write_back):
        h = h_ref[...]
        gate = jnp.dot(h, wg_ref[0].astype(BF16), preferred_element_type=F32)
        up = jnp.dot(h, wu_ref[0].astype(BF16), preferred_element_type=F32)
        act = (gate * jax.nn.sigmoid(gate) * up).astype(BF16)
        wd = wd_ref[0].astype(BF16)
        if not write_back:
            acc_ref[...] += jnp.dot(act, wd, preferred_element_type=F32)
            return
        for r in range(FFN_STORE_CHUNKS):
            rows = slice(r * store_rows, (r + 1) * store_rows)
            acc_ref[rows, :] += jnp.dot(act[rows, :], wd, preferred_element_type=F32)
            store_o(r).start()
        for r in range(FFN_STORE_CHUNKS):
            store_o(r).wait()

    is_last = j == pl.num_programs(1) - 1
    pl.when(jnp.logical_not(is_last))(functools.partial(step, False))
    pl.when(is_last)(functools.partial(step, True))


def _ffn(x2, g, wg_all, wu_all, wd_all, layer, tm=1024, tf=256, norm_rows=128):
    m, d = x2.shape
    f = wg_all.shape[2]
    return pl.pallas_call(
        functools.partial(_ffn_kernel, tm=tm, norm_rows=norm_rows),
        grid=(m // tm, f // tf),
        in_specs=[pl.BlockSpec(memory_space=pl.ANY),
                  pl.BlockSpec((1, d), lambda i, j: (0, 0)),
                  pl.BlockSpec((1, d, tf), lambda i, j: (layer, 0, j)),
                  pl.BlockSpec((1, d, tf), lambda i, j: (layer, 0, j)),
                  pl.BlockSpec((1, tf, d), lambda i, j: (layer, j, 0))],
        out_specs=pl.BlockSpec(memory_space=pl.ANY),
        out_shape=jax.ShapeDtypeStruct((m, d), F32),
        scratch_shapes=[pltpu.VMEM((tm, d), F32), pltpu.VMEM((tm, d), BF16),
                        pltpu.SemaphoreType.DMA((tm // norm_rows,)),
                        pltpu.SemaphoreType.DMA((FFN_STORE_CHUNKS,))],
        compiler_params=pltpu.CompilerParams(dimension_semantics=("arbitrary", "arbitrary"),
                                             vmem_limit_bytes=FFN_VMEM_LIMIT),
        name="ffn",
    )(x2, g.reshape(1, d), wg_all, wu_all, wd_all)


def _alibi_slopes(n):
    return jnp.exp2(-8.0 * jnp.arange(1, n + 1, dtype=F32) / n)


def kernel(x, norm1_g, w_in, a_q_g, a_k_g, lambda_q1, lambda_k1, lambda_q2, lambda_k2, a_out_g,
           b_q_g, b_k_g, b_rpb, b_out_g, c_q_g, c_k_g, c_out_g, w_out, norm2_g, w_gate, w_up, w_down):
    batch, seq, d_model = x.shape
    depth = w_in.shape[0]
    n_slots = d_model // HEAD_DIM
    a_heads = n_slots // 8
    b_heads = (n_slots - 2 * a_heads) // 2
    c_heads = n_slots - 2 * a_heads - b_heads
    a_w, b_w, c_w = a_heads * 2 * HEAD_DIM, b_heads * HEAD_DIM, c_heads * HEAD_DIM
    col_a, col_b, col_c = 0, 3 * a_w, 3 * a_w + 3 * b_w
    tn = 512
    norm_tiles = tuple((c // tn, (c + 2 * w) // tn) for c, w in ((col_a, a_w), (col_b, b_w), (col_c, c_w)))
    scale = HEAD_DIM ** -0.5 * LOG2E
    slopes_a = _alibi_slopes(a_heads) * LOG2E
    slopes_c = _alibi_slopes(c_heads) * LOG2E

    x2 = x.reshape(batch * seq, d_model)
    for l in range(depth):
        lam_init = 0.8 - 0.6 * math.exp(-0.3 * l)
        gcols = jnp.concatenate([
            jnp.tile(a_q_g[l] * scale, 2 * a_heads), jnp.tile(a_k_g[l], 2 * a_heads), jnp.ones((a_w,), F32),
            jnp.tile(b_q_g[l] * scale, b_heads), jnp.tile(b_k_g[l], b_heads), jnp.ones((b_w,), F32),
            jnp.tile(c_q_g[l] * scale, c_heads), jnp.tile(c_k_g[l], c_heads), jnp.ones((c_w,), F32),
        ]).astype(F32).reshape(1, -1)
        lamv = jnp.stack([lambda_q1[l], lambda_k1[l], lambda_q2[l], lambda_k2[l]]).astype(F32)

        proj = _inproj(x2, norm1_g[l], w_in, l, gcols, norm_tiles, tn=tn)
        ma = _attn_a(proj, slopes_a, lamv, a_out_g[l], batch=batch, seq=seq, heads=a_heads,
                     col0=col_a, lam_init=lam_init)
        mb = _attn_b(proj, _nbr_bias_table(b_rpb[l]), b_out_g[l], batch=batch, seq=seq,
                     heads=b_heads, col0=col_b)
        mc = _attn_c(proj, slopes_c, c_out_g[l], batch=batch, seq=seq, heads=c_heads, col0=col_c)
        x2 = _outproj(ma, mb, mc, w_out, l, x2)
        x2 = _ffn(x2, norm2_g[l], w_gate, w_up, w_down, l)
    return x2.reshape(batch, seq, d_model)
```
